```python
import math
import jax
import jax.numpy as jnp
from jax import lax
import numpy as np

D_MODEL = 1024
BATCH = 4
SEQ = 8192
DEPTH = 2

GLA_HEADS = 4
GLA_DK = 64
GLA_DV = 128
GLA_RANK = 16
GLA_TAU = 16.0
DIFF_HEADS = 4
DIFF_DH = 64
ROT_DIM = DIFF_DH // 4
ROPE_THETA = 500000.0
Q_BLOCK = 128
GDN_HEADS = 4
GDN_DK = 128
GDN_DV = 128
CONV_W = 4
CHUNK = 64
N_BRANCH = 3
D_FF = 2816
N_EXPERTS = 8
TOP_K = 2
D_EXPERT = 3584
EPS = 1e-6
MAX_POS_OFFSET = 4096

GLA_QK = GLA_HEADS * GLA_DK
GLA_V = GLA_HEADS * GLA_DV
DIFF_QK = 2 * DIFF_HEADS * DIFF_DH
DIFF_V = DIFF_HEADS * 2 * DIFF_DH
GDN_QK = GDN_HEADS * GDN_DK
GDN_V = GDN_HEADS * GDN_DV
BRANCH_W = GLA_V
IN_SIZES = (GLA_QK, GLA_QK, GLA_V, GLA_V, GLA_RANK, GLA_RANK,
            DIFF_QK, DIFF_QK, DIFF_V,
            GDN_QK, GDN_QK, GDN_V, GDN_V, GDN_HEADS, GDN_HEADS, GDN_HEADS, GDN_HEADS,
            N_BRANCH * D_MODEL)
IN_SPLITS = tuple(int(s) for s in np.cumsum(IN_SIZES)[:-1])
IN_COLS = int(sum(IN_SIZES))
N_DENSE = (DEPTH + 1) // 2
N_MOE = DEPTH // 2

kernel_name = "hybrid_gla_diffattn_gdn_moe_encoder"


def rmsnorm(x, w):
    xf = x.astype(jnp.float32)
    y = xf * lax.rsqrt(jnp.mean(xf * xf, axis=-1, keepdims=True) + EPS)
    return (y * w.astype(jnp.float32)).astype(x.dtype)


def l2norm(x):
    return x * lax.rsqrt(jnp.sum(x * x, axis=-1, keepdims=True) + EPS)


def to_heads(t, n_heads):
    b, l, _ = t.shape
    return t.reshape(b, l, n_heads, -1).transpose(0, 2, 1, 3).astype(jnp.float32)


def from_heads(t):
    return t.transpose(0, 2, 1, 3)


def flip_seq(t):
    return jnp.flip(t, axis=2)


def partial_rope(x, pos):
    inv = ROPE_THETA ** (-jnp.arange(0, ROT_DIM, 2, dtype=jnp.float32) / ROT_DIM)
    ang = pos.astype(jnp.float32)[..., None] * inv
    cos = jnp.cos(ang)[:, :, None, :]
    sin = jnp.sin(ang)[:, :, None, :]
    xr = x[..., :ROT_DIM].astype(jnp.float32)
    x1, x2 = xr[..., :ROT_DIM // 2], xr[..., ROT_DIM // 2:]
    rot = jnp.concatenate([x1 * cos - x2 * sin, x2 * cos + x1 * sin], axis=-1).astype(x.dtype)
    return jnp.concatenate([rot, x[..., ROT_DIM:]], axis=-1)


def short_conv(x, w):
    return lax.conv_general_dilated(x, w[:, None, :].astype(x.dtype), window_strides=(1,), padding='SAME',
                                    dimension_numbers=('NWC', 'WIO', 'NWC'), feature_group_count=x.shape[-1])


def gla_chunked(q, k, v, g):
    B, H, L, dk = q.shape
    dv = v.shape[-1]
    n = L // CHUNK
    q = q.reshape(B, H, n, CHUNK, dk)
    k = k.reshape(B, H, n, CHUNK, dk)
    v = v.reshape(B, H, n, CHUNK, dv)
    b = jnp.cumsum(g.reshape(B, H, n, CHUNK, dk), axis=3)
    b_mid = b[:, :, :, CHUNK // 2 - 1:CHUNK // 2, :]
    b_last = b[:, :, :, -1:, :]
    incl = jnp.tril(jnp.ones((CHUNK, CHUNK), dtype=bool))
    a = jnp.einsum('bhnid,bhnjd->bhnij', q * jnp.exp(b - b_mid), k * jnp.exp(b_mid - b))
    a = jnp.where(incl, a, 0.0)
    o_intra = jnp.einsum('bhnij,bhnjv->bhniv', a, v)
    kv = jnp.einsum('bhncd,bhncv->bhndv', k * jnp.exp(b_last - b), v)
    decay = jnp.exp(b_last[:, :, :, 0, :])

    def step(s, xs):
        d_n, kv_n = xs
        return d_n[..., None] * s + kv_n, s

    s0 = jnp.zeros((B, H, dk, dv), jnp.float32)
    _, s_prev = lax.scan(step, s0, (jnp.moveaxis(decay, 2, 0), jnp.moveaxis(kv, 2, 0)))
    s_prev = jnp.moveaxis(s_prev, 0, 2)
    o_inter = jnp.einsum('bhncd,bhndv->bhncv', q * jnp.exp(b), s_prev)
    return (o_intra + o_inter).reshape(B, H, L, dv)


def gdn_chunked(q, k, v, g, beta):
    B, H, L, dk = q.shape
    dv = v.shape[-1]
    n = L // CHUNK
    q = q.reshape(B, H, n, CHUNK, dk)
    k = k.reshape(B, H, n, CHUNK, dk)
    v = v.reshape(B, H, n, CHUNK, dv)
    beta = beta.reshape(B, H, n, CHUNK)
    b = jnp.cumsum(g.reshape(B, H, n, CHUNK), axis=-1)
    incl = jnp.tril(jnp.ones((CHUNK, CHUNK), dtype=bool))
    strict = jnp.tril(jnp.ones((CHUNK, CHUNK), dtype=bool), k=-1)
    gamma = jnp.exp(jnp.where(incl, b[..., :, None] - b[..., None, :], -jnp.inf))
    kk = jnp.einsum('bhnid,bhnjd->bhnij', k, k)
    m = jnp.eye(CHUNK, dtype=jnp.float32) + jnp.where(strict, beta[..., :, None] * kk * gamma, 0.0)
    rhs = jnp.concatenate([v * beta[..., None], k * (beta * jnp.exp(b))[..., None]], axis=-1)
    sol = lax.linalg.triangular_solve(m, rhs, left_side=True, lower=True, unit_diagonal=True)
    u, w = sol[..., :dv], sol[..., dv:]
    attn = jnp.einsum('bhnid,bhnjd->bhnij', q, k) * gamma
    q_dec = q * jnp.exp(b)[..., None]
    k_end = k * jnp.exp(b[..., -1:] - b)[..., None]
    dec_last = jnp.exp(b[..., -1])

    def step(s, xs):
        u_n, w_n, q_n, k_n, a_n, d_n = xs
        v_new = u_n - jnp.einsum('bhcd,bhdv->bhcv', w_n, s)
        o_n = jnp.einsum('bhcd,bhdv->bhcv', q_n, s) + jnp.einsum('bhij,bhjv->bhiv', a_n, v_new)
        s = d_n[..., None, None] * s + jnp.einsum('bhcd,bhcv->bhdv', k_n, v_new)
        return s, o_n

    xs = tuple(jnp.moveaxis(t, 2, 0) for t in (u, w, q_dec, k_end, attn, dec_last))
    s0 = jnp.zeros((B, H, dk, dv), jnp.float32)
    _, o = lax.scan(step, s0, xs)
    return jnp.moveaxis(o, 0, 2).reshape(B, H, L, dv)


def diff_attention(q, k, v, lam):
    B, L, H2, d = q.shape
    H = H2 // 2
    scale = d ** -0.5
    qb = q.reshape(B, L // Q_BLOCK, Q_BLOCK, H2, d).transpose(1, 0, 3, 2, 4)
    kt = k.transpose(0, 2, 1, 3)
    vt = v.transpose(0, 2, 1, 3)

    def block(q_blk):
        s = jnp.einsum('bhqd,bhkd->bhqk', q_blk, kt).astype(jnp.float32) * scale
        p = jax.nn.softmax(s, axis=-1).reshape(B, H, 2, Q_BLOCK, L)
        a = p[:, :, 0] - lam * p[:, :, 1]
        return jnp.einsum('bhqk,bhkv->bhqv', a.astype(v.dtype), vt)

    o = lax.map(block, qb)
    return o.transpose(1, 0, 3, 2, 4).reshape(B, L, H, 2 * d)


def hybrid_mixer(h, pos, layer, w_in, gla_gate_up, gla_gate_bias, gla_norm, diff_lambda, diff_norm,
                 gdn_conv, gdn_A_log, gdn_dt_bias, gdn_norm, w_branch, w_out):
    f32 = jnp.float32
    B, L, _ = h.shape
    proj = h @ w_in
    (gq, gk, gv, gr, gwf, gwb, dq, dk, dv, nq, nk, nv, nz, nbf, nbb, naf, nab, mg) = jnp.split(proj, IN_SPLITS, axis=-1)

    q = to_heads(gq, GLA_HEADS) * (GLA_DK ** -0.5)
    k = to_heads(gk, GLA_HEADS)
    v = to_heads(gv, GLA_HEADS)

    def gla_log_gate(w_down, d):
        z = w_down.astype(f32) @ gla_gate_up[d].astype(f32) + gla_gate_bias[d].astype(f32)
        return to_heads(jax.nn.log_sigmoid(z) / GLA_TAU, GLA_HEADS)

    o = gla_chunked(q, k, v, gla_log_gate(gwf, 0)) + flip_seq(
        gla_chunked(flip_seq(q), flip_seq(k), flip_seq(v), flip_seq(gla_log_gate(gwb, 1))))
    o = rmsnorm(from_heads(o), gla_norm).reshape(B, L, GLA_V)
    y_gla = (o * jax.nn.silu(gr.astype(f32))).astype(h.dtype)

    qd = partial_rope(dq.reshape(B, L, 2 * DIFF_HEADS, DIFF_DH), pos)
    kd = partial_rope(dk.reshape(B, L, 2 * DIFF_HEADS, DIFF_DH), pos)
    vd = dv.reshape(B, L, DIFF_HEADS, 2 * DIFF_DH)
    lam_init = 0.8 - 0.6 * math.exp(-0.3 * layer)
    lp = diff_lambda.astype(f32)
    lam = jnp.exp(jnp.sum(lp[0] * lp[1])) - jnp.exp(jnp.sum(lp[2] * lp[3])) + lam_init
    od = diff_attention(qd, kd, vd, lam)
    y_diff = (rmsnorm(od, diff_norm).astype(f32) * (1.0 - lam_init)).reshape(B, L, DIFF_V).astype(h.dtype)

    qkv = jax.nn.silu(short_conv(jnp.concatenate([nq, nk, nv], axis=-1), gdn_conv))
    cq, ck, cv = jnp.split(qkv, [GDN_QK, 2 * GDN_QK], axis=-1)
    q = l2norm(to_heads(cq, GDN_HEADS)) * (GDN_DK ** -0.5)
    k = l2norm(to_heads(ck, GDN_HEADS))
    v = to_heads(cv, GDN_HEADS)

    def gdn_gates(b_logit, a_logit, d):
        beta = jax.nn.sigmoid(b_logit.astype(f32))
        g = -jnp.exp(gdn_A_log[d].astype(f32)) * jax.nn.softplus(a_logit.astype(f32) + gdn_dt_bias[d].astype(f32))
        return g.transpose(0, 2, 1), beta.transpose(0, 2, 1)

    g_f, beta_f = gdn_gates(nbf, naf, 0)
    g_b, beta_b = gdn_gates(nbb, nab, 1)
    o = gdn_chunked(q, k, v, g_f, beta_f) + flip_seq(
        gdn_chunked(flip_seq(q), flip_seq(k), flip_seq(v), flip_seq(g_b), flip_seq(beta_b)))
    z = nz.reshape(B, L, GDN_HEADS, GDN_DV).astype(f32)
    y_gdn = (rmsnorm(from_heads(o), gdn_norm) * jax.nn.silu(z)).reshape(B, L, GDN_V).astype(h.dtype)

    ys = jnp.stack([y_gla, y_diff, y_gdn], axis=2)
    branch = jnp.einsum('blnc,ncd->blnd', ys, w_branch)
    gates = jax.nn.sigmoid(mg.reshape(B, L, N_BRANCH, D_MODEL).astype(f32)).astype(h.dtype)
    merged = jnp.sum(gates * branch, axis=2)
    return merged @ w_out


def swiglu(h, w1, w3, w2):
    return (jax.nn.silu(h @ w1) * (h @ w3)) @ w2


def moe_swiglu(h, router_w, w1, w3, w2):
    B, L, D = h.shape
    t = h.reshape(B * L, D)
    logits = (t @ router_w).astype(jnp.float32)
    top_val, top_idx = lax.top_k(logits, TOP_K)
    top_w = jax.nn.softmax(top_val, axis=-1)
    combine = jnp.einsum('tk,tke->te', top_w, jax.nn.one_hot(top_idx, N_EXPERTS, dtype=jnp.float32)).astype(t.dtype)
    out = jnp.zeros_like(t)
    for e in range(N_EXPERTS):
        out = out + combine[:, e:e + 1] * swiglu(t, w1[e], w3[e], w2[e])
    return out.reshape(B, L, D)


def setup_inputs(seed: int = 0) -> dict:
    key = jax.random.key(seed)
    ks = jax.random.split(key, 32)
    f32 = jnp.float32

    def nrm(k, shape, scale):
        return jax.random.normal(k, shape, f32) * scale

    x = nrm(ks[0], (BATCH, SEQ, D_MODEL), 1.0)
    c = nrm(ks[1], (BATCH, D_MODEL), 1.0)
    positions = (jnp.arange(SEQ, dtype=jnp.int32)[None, :]
                 + jax.random.randint(ks[2], (BATCH, 1), 0, MAX_POS_OFFSET, dtype=jnp.int32))
    adaln_w = nrm(ks[3], (DEPTH, D_MODEL, 6 * D_MODEL), 0.5 * D_MODEL ** -0.5)
    adaln_b = nrm(ks[4], (DEPTH, 6 * D_MODEL), 0.02)
    norm_w = 1.0 + nrm(ks[5], (DEPTH, 4, D_MODEL), 0.05)
    w_in = nrm(ks[6], (DEPTH, D_MODEL, IN_COLS), D_MODEL ** -0.5)
    gla_gate_up = nrm(ks[7], (DEPTH, 2, GLA_RANK, GLA_QK), GLA_RANK ** -0.5)
    gla_gate_bias = nrm(ks[8], (DEPTH, 2, GLA_QK), 0.1)
    gla_norm = 1.0 + nrm(ks[9], (DEPTH, GLA_DV), 0.05)
    diff_lambda = nrm(ks[10], (DEPTH, 4, DIFF_DH), 0.1)
    diff_norm = 1.0 + nrm(ks[11], (DEPTH, 2 * DIFF_DH), 0.05)
    gdn_conv = nrm(ks[12], (DEPTH, CONV_W, 2 * GDN_QK + GDN_V), CONV_W ** -0.5)
    gdn_A_log = jnp.log(jax.random.uniform(ks[13], (DEPTH, 2, GDN_HEADS), f32, 1.0, 16.0))
    dt = jnp.exp(jax.random.uniform(ks[14], (DEPTH, 2, GDN_HEADS), f32, math.log(1e-3), math.log(1e-1)))
    gdn_dt_bias = dt + jnp.log(-jnp.expm1(-dt))
    gdn_norm = 1.0 + nrm(ks[15], (DEPTH, GDN_DV), 0.05)
    w_branch = nrm(ks[16], (DEPTH, N_BRANCH, BRANCH_W, D_MODEL), BRANCH_W ** -0.5)
    w_out = nrm(ks[17], (DEPTH, D_MODEL, D_MODEL), D_MODEL ** -0.5)
    ffn_w1 = nrm(ks[18], (N_DENSE, D_MODEL, D_FF), D_MODEL ** -0.5)
    ffn_w3 = nrm(ks[19], (N_DENSE, D_MODEL, D_FF), D_MODEL ** -0.5)
    ffn_w2 = nrm(ks[20], (N_DENSE, D_FF, D_MODEL), D_FF ** -0.5)
    router_w = nrm(ks[21], (N_MOE, D_MODEL, N_EXPERTS), D_MODEL ** -0.5)
    moe_w1 = nrm(ks[22], (N_MOE, N_EXPERTS, D_MODEL, D_EXPERT), D_MODEL ** -0.5)
    moe_w3 = nrm(ks[23], (N_MOE, N_EXPERTS, D_MODEL, D_EXPERT), D_MODEL ** -0.5)
    moe_w2 = nrm(ks[24], (N_MOE, N_EXPERTS, D_EXPERT, D_MODEL), D_EXPERT ** -0.5)
    return {"x": x, "c": c, "positions": positions, "adaln_w": adaln_w, "adaln_b": adaln_b,
            "norm_w": norm_w, "w_in": w_in, "gla_gate_up": gla_gate_up, "gla_gate_bias": gla_gate_bias,
            "gla_norm": gla_norm, "diff_lambda": diff_lambda, "diff_norm": diff_norm, "gdn_conv": gdn_conv,
            "gdn_A_log": gdn_A_log, "gdn_dt_bias": gdn_dt_bias, "gdn_norm": gdn_norm, "w_branch": w_branch,
            "w_out": w_out, "ffn_w1": ffn_w1, "ffn_w3": ffn_w3, "ffn_w2": ffn_w2, "router_w": router_w,
            "moe_w1": moe_w1, "moe_w3": moe_w3, "moe_w2": moe_w2}


def reference(x, c, positions, adaln_w, adaln_b, norm_w, w_in, gla_gate_up, gla_gate_bias, gla_norm,
              diff_lambda, diff_norm, gdn_conv, gdn_A_log, gdn_dt_bias, gdn_norm, w_branch, w_out,
              ffn_w1, ffn_w3, ffn_w2, router_w, moe_w1, moe_w3, moe_w2):
    mod = jnp.einsum('bd,lde->lbe', jax.nn.silu(c), adaln_w) + adaln_b[:, None, :]
    for layer in range(DEPTH):
        sh1, sc1, g1, sh2, sc2, g2 = jnp.split(mod[layer][:, None, :].astype(x.dtype), 6, axis=-1)
        h = rmsnorm(x, norm_w[layer, 0]) * (1.0 + sc1) + sh1
        y = hybrid_mixer(h, positions, layer, w_in[layer], gla_gate_up[layer], gla_gate_bias[layer],
                         gla_norm[layer], diff_lambda[layer], diff_norm[layer], gdn_conv[layer],
                         gdn_A_log[layer], gdn_dt_bias[layer], gdn_norm[layer], w_branch[layer], w_out[layer])
        x = x + g1 * rmsnorm(y, norm_w[layer, 1])
        h = rmsnorm(x, norm_w[layer, 2]) * (1.0 + sc2) + sh2
        if layer % 2 == 0:
            y = swiglu(h, ffn_w1[layer // 2], ffn_w3[layer // 2], ffn_w2[layer // 2])
        else:
            y = moe_swiglu(h, router_w[layer // 2], moe_w1[layer // 2], moe_w3[layer // 2], moe_w2[layer // 2])
        x = x + g2 * rmsnorm(y, norm_w[layer, 3])
    return x
```

```python
import functools
import math

import jax
import jax.numpy as jnp
import numpy as np
from jax import lax
from jax.experimental import pallas as pl
from jax.experimental.pallas import tpu as pltpu

F32 = jnp.float32
BF16 = jnp.bfloat16

D_MODEL = 1024
GLA_HEADS, GLA_DK, GLA_DV, GLA_RANK, GLA_TAU = 4, 64, 128, 16, 16.0
DIFF_HEADS, DIFF_DH = 4, 64
ROT_DIM = DIFF_DH // 4
ROPE_THETA = 500000.0
GDN_HEADS, GDN_DK, GDN_DV, CONV_W = 4, 128, 128, 4
CHUNK = 64
N_BRANCH = 3
N_EXPERTS, TOP_K = 8, 2
EPS = 1e-6
BRANCH_W = 512

_IN_SIZES = (256, 256, 512, 512, 16, 16, 512, 512, 512, 512, 512, 512, 512, 4, 4, 4, 4, 3 * D_MODEL)
_IN_OFF = tuple(int(v) for v in np.concatenate([[0], np.cumsum(_IN_SIZES)]))

P_DQ, P_DK, P_DV = 0, 512, 1024
P_GV, P_GR = 1536, 2048
P_GQ, P_GK = 2560, 2816
P_NQ, P_NK, P_NV, P_NZ = 3072, 3584, 4096, 4608
P_MG = 5120
P_SMALL = 8192
P_COLS = 8448
S_GWF, S_GWB, S_NBF, S_NBB, S_NAF, S_NAB = 0, 16, 32, 36, 40, 44

VMEM_LIMIT = 56 * 1024 * 1024
LANE = 128


def _cparams(sem):
    return pltpu.CompilerParams(dimension_semantics=sem, vmem_limit_bytes=VMEM_LIMIT)


def _dot(a, b):
    return jnp.dot(a, b, preferred_element_type=F32)


def _dot_nt(a, b):
    return lax.dot_general(a, b, (((1,), (1,)), ((), ())), preferred_element_type=F32)


def _dot_tn(a, b):
    return lax.dot_general(a, b, (((0,), (0,)), ((), ())), preferred_element_type=F32)


def _split3_rows(x):
    hi = x.astype(BF16)
    r1 = x - hi.astype(F32)
    mid = r1.astype(BF16)
    lo = (r1 - mid.astype(F32)).astype(BF16)
    return jnp.concatenate([hi, mid, lo], axis=0)


def _dot_split(a, b):
    a_hi = a.astype(BF16)
    a_lo = (a - a_hi.astype(F32)).astype(BF16)
    b_hi = b.astype(BF16)
    b_lo = (b - b_hi.astype(F32)).astype(BF16)
    return _dot(jnp.concatenate([a_hi, a_lo, a_hi], axis=1), jnp.concatenate([b_hi, b_hi, b_lo], axis=0))


def _sigmoid(x):
    return 1.0 / (1.0 + jnp.exp(-x))


def _silu(x):
    return x * _sigmoid(x)


def _softplus(x):
    return jnp.maximum(x, 0.0) + jnp.log1p(jnp.exp(-jnp.abs(x)))


def _rms_rows(x, w):
    return x * lax.rsqrt(jnp.mean(x * x, axis=-1, keepdims=True) + EPS) * w


def _tri_masks(d):
    r = lax.broadcasted_iota(jnp.int32, (CHUNK, CHUNK), 0)
    c = lax.broadcasted_iota(jnp.int32, (CHUNK, CHUNK), 1)
    lag = (r - c) * (1 - 2 * d)
    return lag >= 0, lag > 0


def _adaln_kernel(c_ref, w_ref, b_ref, o_ref):
    c = c_ref[...]
    o_ref[0] = jnp.dot(_silu(c), w_ref[0], preferred_element_type=F32,
                       precision=lax.Precision.HIGHEST) + b_ref[0]


def _adaln(c_pad, adaln_w, adaln_b):
    depth, d, n = adaln_w.shape
    tn = 1536
    return pl.pallas_call(
        _adaln_kernel,
        grid=(depth, n // tn),
        in_specs=[pl.BlockSpec((8, d), lambda l, j: (0, 0)),
                  pl.BlockSpec((1, d, tn), lambda l, j: (l, 0, j)),
                  pl.BlockSpec((1, 1, tn), lambda l, j: (l, 0, j))],
        out_specs=pl.BlockSpec((1, 8, tn), lambda l, j: (l, 0, j)),
        out_shape=jax.ShapeDtypeStruct((depth, 8, n), F32),
        compiler_params=_cparams(("parallel", "parallel")),
        name="adaln",
    )(c_pad, adaln_w, adaln_b.reshape(depth, 1, n))


def _inproj_kernel(x_ref, nw_ref, sc_ref, sh_ref, w_ref, o_ref, h_ref):
    @pl.when(pl.program_id(1) == 0)
    def _():
        y = _rms_rows(x_ref[...], nw_ref[...])
        h_ref[...] = (y * (1.0 + sc_ref[0]) + sh_ref[0]).astype(BF16)

    o_ref[...] = _dot(h_ref[...], w_ref[...]).astype(o_ref.dtype)


def _inproj(x2, nw, sc, sh, w, seq):
    m, d = x2.shape
    n = w.shape[1]
    tm, tn = 1024, 768
    per_b = seq // tm
    return pl.pallas_call(
        _inproj_kernel,
        grid=(m // tm, n // tn),
        in_specs=[pl.BlockSpec((tm, d), lambda i, j: (i, 0)),
                  pl.BlockSpec((1, d), lambda i, j: (0, 0)),
                  pl.BlockSpec((1, 1, d), lambda i, j: (i // per_b, 0, 0)),
                  pl.BlockSpec((1, 1, d), lambda i, j: (i // per_b, 0, 0)),
                  pl.BlockSpec((d, tn), lambda i, j: (0, j))],
        out_specs=pl.BlockSpec((tm, tn), lambda i, j: (i, j)),
        out_shape=jax.ShapeDtypeStruct((m, n), BF16),
        scratch_shapes=[pltpu.VMEM((tm, d), BF16)],
        compiler_params=_cparams(("parallel", "arbitrary")),
        name="inproj",
    )(x2, nw, sc, sh, w)


GLA_TL = 512


def _gla_kernel(q_ref, k_ref, v_ref, sm_ref, up_ref, bias_ref, o_ref, st_ref):
    d = pl.program_id(0)
    nc = GLA_TL // CHUNK

    @pl.when(pl.program_id(2) == 0)
    def _():
        st_ref[...] = jnp.zeros_like(st_ref)

    incl, _ = _tri_masks(d)
    tri = jnp.where(incl, 1.0, 0.0).astype(BF16)
    tri3 = jnp.concatenate([tri, tri, tri], axis=1)
    fwd = d == 0
    scale = GLA_DK ** -0.5

    def chunk(c, carry):
        ce = jnp.where(fwd, c, nc - 1 - c)
        r0 = pl.multiple_of(ce * CHUNK, CHUNK)
        rows = pl.ds(r0, CHUNK)
        sm = sm_ref[0, rows, :]
        z = _dot(jnp.concatenate([sm, sm], axis=1), up_ref[0]) + bias_ref[0]
        g = (jnp.minimum(z, 0.0) - jnp.log1p(jnp.exp(-jnp.abs(z)))) * (1.0 / GLA_TAU)
        b = _dot(tri3, _split3_rows(g))
        b_last = jnp.where(fwd, b[CHUNK - 1:CHUNK], b[0:1])
        b_mid = jnp.where(fwd, b[CHUNK // 2 - 1:CHUNK // 2], b[CHUNK // 2:CHUNK // 2 + 1])
        q = q_ref[0, rows, :].astype(F32) * scale
        k = k_ref[0, rows, :].astype(F32)
        qe = (q * jnp.exp(b - b_mid)).astype(BF16)
        ke = (k * jnp.exp(b_mid - b)).astype(BF16)
        kend = (k * jnp.exp(b_last - b)).astype(BF16)
        qd = (q * jnp.exp(b)).astype(BF16)
        st = st_ref[...]
        st_b = st.astype(BF16)
        new_cols = []
        for h in range(GLA_HEADS):
            ks = slice(h * GLA_DK, (h + 1) * GLA_DK)
            vs = slice(h * GLA_DV, (h + 1) * GLA_DV)
            v_h = v_ref[0, rows, vs]
            a = jnp.where(incl, _dot_nt(qe[:, ks], ke[:, ks]), 0.0)
            o = _dot(a.astype(BF16), v_h) + _dot_nt(qd[:, ks], st_b[:, ks])
            o_ref[0, 0, rows, vs] = o.astype(o_ref.dtype)
            new_cols.append(_dot_tn(v_h, kend[:, ks]))
        st_ref[...] = st * jnp.exp(b_last) + jnp.concatenate(new_cols, axis=1)
        return carry

    lax.fori_loop(0, nc, chunk, 0)


def _gla(p3, up2, bias, batch, seq):
    nblk = seq // GLA_TL

    def blk(dd, j):
        return j + dd * (nblk - 1 - 2 * j)

    return pl.pallas_call(
        _gla_kernel,
        grid=(2, batch, nblk),
        in_specs=[pl.BlockSpec((1, GLA_TL, 256), lambda dd, b, j: (b, blk(dd, j), P_GQ // 256)),
                  pl.BlockSpec((1, GLA_TL, 256), lambda dd, b, j: (b, blk(dd, j), P_GK // 256)),
                  pl.BlockSpec((1, GLA_TL, 512), lambda dd, b, j: (b, blk(dd, j), P_GV // 512)),
                  pl.BlockSpec((1, GLA_TL, LANE), lambda dd, b, j: (b, blk(dd, j), P_SMALL // LANE)),
                  pl.BlockSpec((1, 2 * LANE, 256), lambda dd, b, j: (dd, 0, 0)),
                  pl.BlockSpec((1, 1, 256), lambda dd, b, j: (dd, 0, 0))],
        out_specs=pl.BlockSpec((1, 1, GLA_TL, 512), lambda dd, b, j: (dd, b, blk(dd, j), 0)),
        out_shape=jax.ShapeDtypeStruct((2, batch, seq, 512), BF16),
        scratch_shapes=[pltpu.VMEM((GLA_DV, GLA_HEADS * GLA_DK), F32)],
        compiler_params=_cparams(("parallel", "parallel", "arbitrary")),
        name="gla",
    )(p3, p3, p3, p3, up2, bias)


ROPE_TL = 512


def _rope_kernel(x_ref, pos_ref, inv_ref, o_ref):
    s = pl.program_id(2)
    ang = pos_ref[0].astype(F32) * inv_ref[...]
    cosf = jnp.cos(ang)
    sinf = jnp.sin(ang)
    lane = lax.broadcasted_iota(jnp.int32, (1, LANE), 1) % DIFF_DH
    half = ROT_DIM // 2
    cmul = jnp.where(lane < ROT_DIM, cosf, 1.0)
    s_up = jnp.where(lane < half, -sinf, 0.0)
    s_dn = jnp.where((lane >= half) & (lane < ROT_DIM), sinf, 0.0)
    scale = jnp.where(s == 0, DIFF_DH ** -0.5, 1.0)
    for gidx in range(4):
        cs = slice(gidx * LANE, (gidx + 1) * LANE)
        x = x_ref[0, :, cs].astype(F32)
        y = x * cmul + pltpu.roll(x, LANE - half, 1) * s_up + pltpu.roll(x, half, 1) * s_dn
        o_ref[0, 0, :, cs] = (y * scale).astype(o_ref.dtype)


def _rope(p3, pos3, inv_row, batch, seq):
    nblk = seq // ROPE_TL
    return pl.pallas_call(
        _rope_kernel,
        grid=(batch, nblk, 2),
        in_specs=[pl.BlockSpec((1, ROPE_TL, 512), lambda b, j, s: (b, j, s)),
                  pl.BlockSpec((1, ROPE_TL, 1), lambda b, j, s: (b, j, 0)),
                  pl.BlockSpec((1, LANE), lambda b, j, s: (0, 0))],
        out_specs=pl.BlockSpec((1, 1, ROPE_TL, 512), lambda b, j, s: (s, b, j, 0)),
        out_shape=jax.ShapeDtypeStruct((2, batch, seq, 512), BF16),
        compiler_params=_cparams(("parallel", "parallel", "arbitrary")),
        name="rope",
    )(p3, pos3, inv_row)


ATT_TQ = 512
ATT_TK = 512


def _attn_kernel(q_ref, k_ref, v_ref, lam_ref, o_ref, *, lam_init):
    q = q_ref[0, 0]
    lane = lax.broadcasted_iota(jnp.int32, q.shape, 1)
    zero = jnp.zeros_like(q)
    q1 = jnp.where(lane < DIFF_DH, q, zero)
    q2 = jnp.where(lane >= DIFF_DH, q, zero)
    tq = q.shape[0]
    nk = k_ref.shape[2] // ATT_TK

    def one(qm, k, v, m, l, acc):
        s = _dot_nt(qm, k)
        m_new = jnp.maximum(m, jnp.max(s, axis=-1, keepdims=True))
        p = jnp.exp(s - m_new)
        alpha = jnp.exp(m - m_new)
        l_new = alpha * l + jnp.sum(p, axis=-1, keepdims=True)
        acc_new = alpha * acc + _dot(p.astype(BF16), v)
        return m_new, l_new, acc_new

    def body(t, carry):
        m1, l1, a1, m2, l2, a2 = carry
        r0 = pl.multiple_of(t * ATT_TK, ATT_TK)
        k = k_ref[0, 0, pl.ds(r0, ATT_TK), :]
        v = v_ref[0, pl.ds(r0, ATT_TK), :]
        m1, l1, a1 = one(q1, k, v, m1, l1, a1)
        m2, l2, a2 = one(q2, k, v, m2, l2, a2)
        return m1, l1, a1, m2, l2, a2

    neg = jnp.full((tq, 1), -jnp.inf, F32)
    zl = jnp.zeros((tq, 1), F32)
    za = jnp.zeros((tq, LANE), F32)
    m1, l1, a1, m2, l2, a2 = lax.fori_loop(0, nk, body, (neg, zl, za, neg, zl, za))
    lp = lam_ref[...]
    lam = (jnp.exp(jnp.sum(lp[0:1] * lp[1:2], axis=-1, keepdims=True))
           - jnp.exp(jnp.sum(lp[2:3] * lp[3:4], axis=-1, keepdims=True)) + lam_init)
    o_ref[0] = (a1 / l1 - lam * (a2 / l2)).astype(o_ref.dtype)


def _attn(qk, p3, lam_p, lam_init, batch, seq):
    nq = seq // ATT_TQ
    return pl.pallas_call(
        functools.partial(_attn_kernel, lam_init=lam_init),
        grid=(batch, DIFF_HEADS, nq),
        in_specs=[pl.BlockSpec((1, 1, ATT_TQ, LANE), lambda b, h, i: (0, b, i, h)),
                  pl.BlockSpec((1, 1, seq, LANE), lambda b, h, i: (1, b, 0, h)),
                  pl.BlockSpec((1, seq, LANE), lambda b, h, i: (b, 0, P_DV // LANE + h)),
                  pl.BlockSpec((4, DIFF_DH), lambda b, h, i: (0, 0))],
        out_specs=pl.BlockSpec((1, ATT_TQ, LANE), lambda b, h, i: (b, i, h)),
        out_shape=jax.ShapeDtypeStruct((batch, seq, 512), BF16),
        compiler_params=_cparams(("parallel", "parallel", "arbitrary")),
        name="diffattn",
    )(qk, qk, p3, lam_p)


GPREP_TL = 512
HALO = 16


def _gdn_prep_kernel(x_ref, prev_ref, next_ref, w_ref, o_ref, xe_ref):
    j = pl.program_id(1)
    s = pl.program_id(2)
    nblk = pl.num_programs(1)
    tl = GPREP_TL
    prev = prev_ref[0].astype(F32)
    nxt = next_ref[0].astype(F32)
    xe_ref[0:HALO, :] = jnp.where(j == 0, 0.0, prev)
    xe_ref[HALO:HALO + tl, :] = x_ref[0].astype(F32)
    xe_ref[HALO + tl:2 * HALO + tl, :] = jnp.where(j == nblk - 1, 0.0, nxt)
    w = w_ref[...]
    y = (xe_ref[pl.ds(HALO - 1, tl), :] * w[0:1] + xe_ref[pl.ds(HALO, tl), :] * w[1:2]
         + xe_ref[pl.ds(HALO + 1, tl), :] * w[2:3] + xe_ref[pl.ds(HALO + 2, tl), :] * w[3:4])
    y = _silu(y)
    qscale = jnp.where(s == 0, GDN_DK ** -0.5, 1.0)
    for h in range(GDN_HEADS):
        cs = slice(h * GDN_DK, (h + 1) * GDN_DK)
        yh = y[:, cs]
        nrm = lax.rsqrt(jnp.sum(yh * yh, axis=-1, keepdims=True) + EPS) * qscale
        fac = jnp.where(s == 2, 1.0, nrm)
        o_ref[0, 0, :, cs] = (yh * fac).astype(o_ref.dtype)


def _gdn_prep(p3, conv_w, batch, seq):
    nblk = seq // GPREP_TL
    hb = GPREP_TL // HALO
    last = seq // HALO - 1
    base = P_NQ // 512
    return pl.pallas_call(
        _gdn_prep_kernel,
        grid=(batch, nblk, 3),
        in_specs=[pl.BlockSpec((1, GPREP_TL, 512), lambda b, j, s: (b, j, base + s)),
                  pl.BlockSpec((1, HALO, 512), lambda b, j, s: (b, jnp.maximum(j * hb - 1, 0), base + s)),
                  pl.BlockSpec((1, HALO, 512), lambda b, j, s: (b, jnp.minimum((j + 1) * hb, last), base + s)),
                  pl.BlockSpec((CONV_W, 512), lambda b, j, s: (0, s))],
        out_specs=pl.BlockSpec((1, 1, GPREP_TL, 512), lambda b, j, s: (s, b, j, 0)),
        out_shape=jax.ShapeDtypeStruct((3, batch, seq, 512), BF16),
        scratch_shapes=[pltpu.VMEM((GPREP_TL + 2 * HALO, 512), F32)],
        compiler_params=_cparams(("parallel", "parallel", "arbitrary")),
        name="gdn_prep",
    )(p3, p3, p3, conv_w)


GDN_TL = 512


def _gdn_kernel(q_ref, k_ref, v_ref, sm_ref, sel_ref, nega_ref, dtb_ref, o_ref, s_ref):
    d = pl.program_id(0)
    nc = GDN_TL // CHUNK

    @pl.when(pl.program_id(2) == 0)
    def _():
        s_ref[...] = jnp.zeros_like(s_ref)

    incl, strict = _tri_masks(d)
    tri = jnp.where(incl, 1.0, 0.0).astype(BF16)
    tri3 = jnp.concatenate([tri, tri, tri], axis=1)
    upper = jnp.where(strict, 1.0, 0.0)
    r_i = lax.broadcasted_iota(jnp.int32, (CHUNK, CHUNK), 0)
    c_i = lax.broadcasted_iota(jnp.int32, (CHUNK, CHUNK), 1)
    eye = jnp.where(r_i == c_i, 1.0, 0.0)
    fwd = d == 0
    H = GDN_HEADS

    def chunk(c, carry):
        ce = jnp.where(fwd, c, nc - 1 - c)
        r0 = pl.multiple_of(ce * CHUNK, CHUNK)
        rows = pl.ds(r0, CHUNK)
        sel = _dot(sm_ref[0, rows, :], sel_ref[0])
        beta = _sigmoid(sel)
        g = nega_ref[0] * _softplus(sel + dtb_ref[0])
        g2 = jnp.concatenate([g[:, H + h:H + h + 1] * upper for h in range(H)] + [g], axis=1)
        dd = _dot(tri3, _split3_rows(g2))
        bcum = dd[:, H * CHUNK:]
        for h in range(H):
            cs = slice(h * GDN_DK, (h + 1) * GDN_DK)
            kh = k_ref[0, 0, rows, cs]
            qh = q_ref[0, 0, rows, cs]
            vh = v_ref[0, 0, rows, cs]
            kf = kh.astype(F32)
            gam = jnp.where(incl, jnp.exp(dd[:, h * CHUNK:(h + 1) * CHUNK]), 0.0)
            bcol = bcum[:, H + h:H + h + 1]
            blast = jnp.where(fwd, bcol[CHUNK - 1:CHUNK], bcol[0:1])
            eb = jnp.exp(bcol)
            bet = beta[:, h:h + 1]
            kk = _dot_nt(kh, kh)
            n = jnp.where(strict, -(bet * kk * gam), 0.0)
            t = eye + n
            pw = n
            for _ in range(5):
                pw = _dot_split(pw, pw)
                t = t + _dot_split(t, pw)
            rhs = jnp.concatenate([vh.astype(F32) * bet, kf * (bet * eb)], axis=1).astype(BF16)
            sol = _dot(t.astype(BF16), rhs)
            u = sol[:, :GDN_DV]
            w = sol[:, GDN_DV:]
            attn = _dot_nt(qh, kh) * gam
            qdec = (qh.astype(F32) * eb).astype(BF16)
            kend = (kf * jnp.exp(blast - bcol)).astype(BF16)
            st = s_ref[h]
            st_b = st.astype(BF16)
            vnew = u - _dot(w.astype(BF16), st_b)
            vnb = vnew.astype(BF16)
            o = _dot(qdec, st_b) + _dot(attn.astype(BF16), vnb)
            o_ref[0, 0, rows, cs] = o.astype(o_ref.dtype)
            s_ref[h] = jnp.exp(blast) * st + _dot_tn(kend, vnb)
        return carry

    lax.fori_loop(0, nc, chunk, 0)


def _gdn(qkv, p3, sel_m, nega, dtb, batch, seq):
    nblk = seq // GDN_TL

    def blk(dd, j):
        return j + dd * (nblk - 1 - 2 * j)

    def qspec(idx):
        return pl.BlockSpec((1, 1, GDN_TL, 512), lambda dd, b, j: (idx, b, blk(dd, j), 0))

    return pl.pallas_call(
        _gdn_kernel,
        grid=(2, batch, nblk),
        in_specs=[qspec(0), qspec(1), qspec(2),
                  pl.BlockSpec((1, GDN_TL, LANE), lambda dd, b, j: (b, blk(dd, j), P_SMALL // LANE)),
                  pl.BlockSpec((1, LANE, LANE), lambda dd, b, j: (dd, 0, 0)),
                  pl.BlockSpec((1, 1, LANE), lambda dd, b, j: (dd, 0, 0)),
                  pl.BlockSpec((1, 1, LANE), lambda dd, b, j: (dd, 0, 0))],
        out_specs=pl.BlockSpec((1, 1, GDN_TL, 512), lambda dd, b, j: (dd, b, blk(dd, j), 0)),
        out_shape=jax.ShapeDtypeStruct((2, batch, seq, 512), BF16),
        scratch_shapes=[pltpu.VMEM((GDN_HEADS, GDN_DK, GDN_DV), F32)],
        compiler_params=_cparams(("parallel", "parallel", "arbitrary")),
        name="gdn",
    )(qkv, qkv, qkv, p3, sel_m, nega, dtb)


MERGE_TM = 512


def _head_rms(o, w):
    parts = []
    for h in range(4):
        oh = o[:, h * LANE:(h + 1) * LANE]
        parts.append(oh * lax.rsqrt(jnp.mean(oh * oh, axis=-1, keepdims=True) + EPS))
    return jnp.concatenate(parts, axis=1) * w


def _merge_kernel(gf_ref, gb_ref, gr_ref, od_ref, nf_ref, nb_ref, nz_ref, mg0_ref, mg1_ref, mg2_ref,
                  x_ref, g1_ref, nrm_ref, wb_ref, wo_ref, nw_ref, o_ref, *, diff_scale):
    nrm = nrm_ref[...]
    y_gla = _head_rms(gf_ref[0, 0].astype(F32) + gb_ref[0, 0].astype(F32), nrm[0:1]) * _silu(gr_ref[0].astype(F32))
    y_diff = _head_rms(od_ref[0].astype(F32), nrm[1:2]) * diff_scale
    y_gdn = _head_rms(nf_ref[0, 0].astype(F32) + nb_ref[0, 0].astype(F32), nrm[2:3]) * _silu(nz_ref[0].astype(F32))
    merged = (_sigmoid(mg0_ref[0].astype(F32)) * _dot(y_gla.astype(BF16), wb_ref[0])
              + _sigmoid(mg1_ref[0].astype(F32)) * _dot(y_diff.astype(BF16), wb_ref[1])
              + _sigmoid(mg2_ref[0].astype(F32)) * _dot(y_gdn.astype(BF16), wb_ref[2]))
    y = _dot(merged.astype(BF16), wo_ref[...])
    o_ref[0] = x_ref[0] + g1_ref[0] * _rms_rows(y, nw_ref[...])


def _merge(gla_o, p3, att_o, gdn_o, x3, g1, nrm3, wb, wo, nw, diff_scale, batch, seq):
    tm = MERGE_TM
    nblk = seq // tm
    d = D_MODEL

    def dirspec(idx):
        return pl.BlockSpec((1, 1, tm, 512), lambda b, j: (idx, b, j, 0))

    def pspec(width, col):
        return pl.BlockSpec((1, tm, width), lambda b, j: (b, j, col // width))

    return pl.pallas_call(
        functools.partial(_merge_kernel, diff_scale=diff_scale),
        grid=(batch, nblk),
        in_specs=[dirspec(0), dirspec(1), pspec(512, P_GR),
                  pl.BlockSpec((1, tm, 512), lambda b, j: (b, j, 0)),
                  dirspec(0), dirspec(1), pspec(512, P_NZ),
                  pspec(d, P_MG), pspec(d, P_MG + d), pspec(d, P_MG + 2 * d),
                  pl.BlockSpec((1, tm, d), lambda b, j: (b, j, 0)),
                  pl.BlockSpec((1, 1, d), lambda b, j: (b, 0, 0)),
                  pl.BlockSpec((3, 512), lambda b, j: (0, 0)),
                  pl.BlockSpec((3, 512, d), lambda b, j: (0, 0, 0)),
                  pl.BlockSpec((d, d), lambda b, j: (0, 0)),
                  pl.BlockSpec((1, d), lambda b, j: (0, 0))],
        out_specs=pl.BlockSpec((1, tm, d), lambda b, j: (b, j, 0)),
        out_shape=jax.ShapeDtypeStruct((batch, seq, d), F32),
        compiler_params=_cparams(("parallel", "parallel")),
        name="merge",
    )(gla_o, gla_o, p3, att_o, gdn_o, gdn_o, p3, p3, p3, p3, x3, g1, nrm3, wb, wo, nw)


FFN_TM = 512


def _ffn_kernel(x_ref, nw_ref, sc_ref, sh_ref, w1_ref, w3_ref, w2_ref, g2_ref, nwo_ref, o_ref, h_ref, acc_ref):
    j = pl.program_id(1)

    @pl.when(j == 0)
    def _():
        y = _rms_rows(x_ref[...], nw_ref[...])
        h_ref[...] = (y * (1.0 + sc_ref[0]) + sh_ref[0]).astype(BF16)
        acc_ref[...] = jnp.zeros_like(acc_ref)

    h = h_ref[...]
    a = _dot(h, w1_ref[...])
    b = _dot(h, w3_ref[...])
    acc_ref[...] += _dot((_silu(a) * b).astype(BF16), w2_ref[...])

    @pl.when(j == pl.num_programs(1) - 1)
    def _():
        o_ref[...] = x_ref[...] + g2_ref[0] * _rms_rows(acc_ref[...], nwo_ref[...])


def _ffn(x2, nw, sc, sh, w1, w3, w2, g2, nwo, seq):
    m, d = x2.shape
    f = w1.shape[1]
    tm, tf = FFN_TM, f // 2
    per_b = seq // tm
    vec = pl.BlockSpec((1, d), lambda i, j: (0, 0))
    mod = pl.BlockSpec((1, 1, d), lambda i, j: (i // per_b, 0, 0))
    return pl.pallas_call(
        _ffn_kernel,
        grid=(m // tm, f // tf),
        in_specs=[pl.BlockSpec((tm, d), lambda i, j: (i, 0)), vec, mod, mod,
                  pl.BlockSpec((d, tf), lambda i, j: (0, j)),
                  pl.BlockSpec((d, tf), lambda i, j: (0, j)),
                  pl.BlockSpec((tf, d), lambda i, j: (j, 0)),
                  mod, vec],
        out_specs=pl.BlockSpec((tm, d), lambda i, j: (i, 0)),
        out_shape=jax.ShapeDtypeStruct((m, d), F32),
        scratch_shapes=[pltpu.VMEM((tm, d), BF16), pltpu.VMEM((tm, d), F32)],
        compiler_params=_cparams(("parallel", "arbitrary")),
        name="ffn",
    )(x2, nw, sc, sh, w1, w3, w2, g2, nwo)


ROUTER_TM = 1024


def _router_kernel(x_ref, nw_ref, sc_ref, sh_ref, rw_ref, o_ref):
    y = _rms_rows(x_ref[...], nw_ref[...])
    h = y * (1.0 + sc_ref[0]) + sh_ref[0]
    logits = jnp.dot(h, rw_ref[...], preferred_element_type=F32, precision=lax.Precision.HIGHEST)
    lane = lax.broadcasted_iota(jnp.int32, logits.shape, 1)
    neg = -jnp.inf
    logits = jnp.where(lane < N_EXPERTS, logits, neg)
    m1 = jnp.max(logits, axis=-1, keepdims=True)
    i1 = jnp.min(jnp.where(logits == m1, lane, LANE), axis=-1, keepdims=True)
    rest = jnp.where(lane == i1, neg, logits)
    m2 = jnp.max(rest, axis=-1, keepdims=True)
    i2 = jnp.min(jnp.where(rest == m2, lane, LANE), axis=-1, keepdims=True)
    e = jnp.exp(m2 - m1)
    w1 = 1.0 / (1.0 + e)
    w2 = e / (1.0 + e)
    o_ref[...] = jnp.where(lane == i1, w1, 0.0) + jnp.where(lane == i2, w2, 0.0)


def _router(x2, nw, sc, sh, rw_pad, seq):
    m, d = x2.shape
    tm = ROUTER_TM
    per_b = seq // tm
    return pl.pallas_call(
        _router_kernel,
        grid=(m // tm,),
        in_specs=[pl.BlockSpec((tm, d), lambda i: (i, 0)),
                  pl.BlockSpec((1, d), lambda i: (0, 0)),
                  pl.BlockSpec((1, 1, d), lambda i: (i // per_b, 0, 0)),
                  pl.BlockSpec((1, 1, d), lambda i: (i // per_b, 0, 0)),
                  pl.BlockSpec((d, LANE), lambda i: (0, 0))],
        out_specs=pl.BlockSpec((tm, LANE), lambda i: (i, 0)),
        out_shape=jax.ShapeDtypeStruct((m, LANE), F32),
        compiler_params=_cparams(("parallel",)),
        name="router",
    )(x2, nw, sc, sh, rw_pad)


MOE_TM = 1024
MOE_TF = 512


def _moe_kernel(x_ref, nw_ref, sc_ref, sh_ref, cw_ref, w1_ref, w3_ref, w2_ref, g2_ref, nwo_ref, o_ref,
                h_ref, acc_ref):
    e = pl.program_id(1)
    j = pl.program_id(2)
    first = (e == 0) & (j == 0)
    last = (e == pl.num_programs(1) - 1) & (j == pl.num_programs(2) - 1)

    @pl.when(first)
    def _():
        y = _rms_rows(x_ref[...], nw_ref[...])
        h_ref[...] = (y * (1.0 + sc_ref[0]) + sh_ref[0]).astype(BF16)
        acc_ref[...] = jnp.zeros_like(acc_ref)

    cw = cw_ref[...]
    lane = lax.broadcasted_iota(jnp.int32, cw.shape, 1)
    ce = jnp.sum(jnp.where(lane == e, cw, 0.0), axis=-1, keepdims=True)
    h = h_ref[...]
    a = _dot(h, w1_ref[0])
    b = _dot(h, w3_ref[0])
    acc_ref[...] += _dot((_silu(a) * b * ce).astype(BF16), w2_ref[0])

    @pl.when(last)
    def _():
        o_ref[...] = x_ref[...] + g2_ref[0] * _rms_rows(acc_ref[...], nwo_ref[...])


def _moe(x2, nw, sc, sh, cw, w1, w3, w2, g2, nwo, seq):
    m, d = x2.shape
    ne, _, f = w1.shape
    tm, tf = MOE_TM, MOE_TF
    per_b = seq // tm
    vec = pl.BlockSpec((1, d), lambda i, e, j: (0, 0))
    mod = pl.BlockSpec((1, 1, d), lambda i, e, j: (i // per_b, 0, 0))
    return pl.pallas_call(
        _moe_kernel,
        grid=(m // tm, ne, f // tf),
        in_specs=[pl.BlockSpec((tm, d), lambda i, e, j: (i, 0)), vec, mod, mod,
                  pl.BlockSpec((tm, LANE), lambda i, e, j: (i, 0)),
                  pl.BlockSpec((1, d, tf), lambda i, e, j: (e, 0, j)),
                  pl.BlockSpec((1, d, tf), lambda i, e, j: (e, 0, j)),
                  pl.BlockSpec((1, tf, d), lambda i, e, j: (e, j, 0)),
                  mod, vec],
        out_specs=pl.BlockSpec((tm, d), lambda i, e, j: (i, 0)),
        out_shape=jax.ShapeDtypeStruct((m, d), F32),
        scratch_shapes=[pltpu.VMEM((tm, d), BF16), pltpu.VMEM((tm, d), F32)],
        compiler_params=_cparams(("parallel", "arbitrary", "arbitrary")),
        name="moe",
    )(x2, nw, sc, sh, cw, w1, w3, w2, g2, nwo)


def _reorder_w_in(w):
    o = _IN_OFF

    def cols(a, b):
        return w[:, o[a]:o[b]]

    small = jnp.concatenate([cols(4, 6), cols(13, 17)], axis=1)
    pad = jnp.zeros((w.shape[0], P_COLS - P_SMALL - small.shape[1]), w.dtype)
    out = jnp.concatenate([cols(6, 9),
                           cols(2, 4),
                           cols(0, 2),
                           cols(9, 13),
                           cols(17, 18),
                           small, pad], axis=1)
    return out.astype(BF16)


def _gla_gate_params(up, bias):
    hi = up.astype(BF16)
    lo = (up - hi.astype(F32)).astype(BF16)
    out = jnp.zeros((2, 2 * LANE, GLA_HEADS * GLA_DK), BF16)
    for dd, off in ((0, S_GWF), (1, S_GWB)):
        out = out.at[dd, off:off + GLA_RANK].set(hi[dd])
        out = out.at[dd, LANE + off:LANE + off + GLA_RANK].set(lo[dd])
    return out, bias.reshape(2, 1, -1).astype(F32)


def _gdn_gate_params(a_log, dt_bias):
    h = GDN_HEADS
    sel = np.zeros((2, LANE, LANE), np.float32)
    for dd, (boff, aoff) in enumerate(((S_NBF, S_NAF), (S_NBB, S_NAB))):
        for i in range(h):
            sel[dd, boff + i, i] = 1.0
            sel[dd, aoff + i, h + i] = 1.0
    nega = jnp.zeros((2, 1, LANE), F32).at[:, 0, h:2 * h].set(-jnp.exp(a_log.astype(F32)))
    dtb = jnp.zeros((2, 1, LANE), F32).at[:, 0, h:2 * h].set(dt_bias.astype(F32))
    return jnp.asarray(sel, BF16), nega, dtb


def _rope_inv_row():
    inv = ROPE_THETA ** (-jnp.arange(0, ROT_DIM, 2, dtype=F32) / ROT_DIM)
    lane = np.arange(LANE) % DIFF_DH
    row = jnp.where(jnp.asarray(lane < ROT_DIM), inv[jnp.asarray(lane % (ROT_DIM // 2))], 0.0)
    return row.reshape(1, LANE).astype(F32)


def kernel(x, c, positions, adaln_w, adaln_b, norm_w, w_in, gla_gate_up, gla_gate_bias, gla_norm, diff_lambda,
           diff_norm, gdn_conv, gdn_A_log, gdn_dt_bias, gdn_norm, w_branch, w_out, ffn_w1, ffn_w3, ffn_w2,
           router_w, moe_w1, moe_w3, moe_w2):
    batch, seq, d = x.shape
    depth = adaln_w.shape[0]
    m = batch * seq

    c_pad = jnp.zeros((8, d), F32).at[:batch].set(c)
    mod = _adaln(c_pad, adaln_w, adaln_b)[:, :batch]
    pos3 = positions.reshape(batch, seq, 1)
    inv_row = _rope_inv_row()

    for layer in range(depth):
        sh1, sc1, g1, sh2, sc2, g2 = [t.reshape(batch, 1, d) for t in jnp.split(mod[layer], 6, axis=-1)]
        nw = norm_w[layer].astype(F32)

        p = _inproj(x.reshape(m, d), nw[0:1], sc1, sh1, _reorder_w_in(w_in[layer]), seq)
        p3 = p.reshape(batch, seq, P_COLS)

        up2, gbias = _gla_gate_params(gla_gate_up[layer], gla_gate_bias[layer])
        gla_o = _gla(p3, up2, gbias, batch, seq)

        qk = _rope(p3, pos3, inv_row, batch, seq)
        lam_init = 0.8 - 0.6 * math.exp(-0.3 * layer)
        att_o = _attn(qk, p3, diff_lambda[layer].astype(F32), lam_init, batch, seq)

        qkv = _gdn_prep(p3, gdn_conv[layer].astype(F32), batch, seq)
        sel_m, nega, dtb = _gdn_gate_params(gdn_A_log[layer], gdn_dt_bias[layer])
        gdn_o = _gdn(qkv, p3, sel_m, nega, dtb, batch, seq)

        nrm3 = jnp.stack([jnp.tile(gla_norm[layer], 4), jnp.tile(diff_norm[layer], 4),
                          jnp.tile(gdn_norm[layer], 4)]).astype(F32)
        x = _merge(gla_o, p3, att_o, gdn_o, x, g1, nrm3, w_branch[layer].astype(BF16),
                   w_out[layer].astype(BF16), nw[1:2], 1.0 - lam_init, batch, seq)

        x2 = x.reshape(m, d)
        if layer % 2 == 0:
            li = layer // 2
            x2 = _ffn(x2, nw[2:3], sc2, sh2, ffn_w1[li].astype(BF16), ffn_w3[li].astype(BF16),
                      ffn_w2[li].astype(BF16), g2, nw[3:4], seq)
        else:
            li = layer // 2
            rw_pad = jnp.zeros((d, LANE), F32).at[:, :N_EXPERTS].set(router_w[li])
            cw = _router(x2, nw[2:3], sc2, sh2, rw_pad, seq)
            x2 = _moe(x2, nw[2:3], sc2, sh2, cw, moe_w1[li].astype(BF16), moe_w3[li].astype(BF16),
                      moe_w2[li].astype(BF16), g2, nw[3:4], seq)
        x = x2.reshape(batch, seq, d)
    return x
```

```python
import functools
import math

import jax
import jax.numpy as jnp
import numpy as np
from jax import lax
from jax.experimental import pallas as pl
from jax.experimental.pallas import tpu as pltpu

F32 = jnp.float32
BF16 = jnp.bfloat16

D_MODEL = 1024
GLA_HEADS, GLA_DK, GLA_DV, GLA_RANK, GLA_TAU = 4, 64, 128, 16, 16.0
DIFF_HEADS, DIFF_DH = 4, 64
ROT_DIM = DIFF_DH // 4
ROPE_THETA = 500000.0
GDN_HEADS, GDN_DK, GDN_DV, CONV_W = 4, 128, 128, 4
CHUNK = 64
N_BRANCH = 3
N_EXPERTS, TOP_K = 8, 2
EPS = 1e-6
BRANCH_W = 512

_IN_SIZES = (256, 256, 512, 512, 16, 16, 512, 512, 512, 512, 512, 512, 512, 4, 4, 4, 4, 3 * D_MODEL)
_IN_OFF = tuple(int(v) for v in np.concatenate([[0], np.cumsum(_IN_SIZES)]))

P_DQ, P_DK, P_DV = 0, 512, 1024
P_GV, P_GR = 1536, 2048
P_GQ, P_GK = 2560, 2816
P_NQ, P_NK, P_NV, P_NZ = 3072, 3584, 4096, 4608
P_MG = 5120
P_SMALL = 8192
P_COLS = 8448
S_GWF, S_GWB, S_NBF, S_NBB, S_NAF, S_NAB = 0, 16, 32, 36, 40, 44

VMEM_LIMIT = 56 * 1024 * 1024
LANE = 128


def _cparams(sem):
    return pltpu.CompilerParams(dimension_semantics=sem, vmem_limit_bytes=VMEM_LIMIT)


def _dot(a, b):
    return jnp.dot(a, b, preferred_element_type=F32)


def _dot_nt(a, b):
    return lax.dot_general(a, b, (((1,), (1,)), ((), ())), preferred_element_type=F32)


def _dot_tn(a, b):
    return lax.dot_general(a, b, (((0,), (0,)), ((), ())), preferred_element_type=F32)


def _split3_rows(x):
    hi = x.astype(BF16)
    r1 = x - hi.astype(F32)
    mid = r1.astype(BF16)
    lo = (r1 - mid.astype(F32)).astype(BF16)
    return jnp.concatenate([hi, mid, lo], axis=0)


def _dot_split(a, b):
    a_hi = a.astype(BF16)
    a_lo = (a - a_hi.astype(F32)).astype(BF16)
    b_hi = b.astype(BF16)
    b_lo = (b - b_hi.astype(F32)).astype(BF16)
    return _dot(jnp.concatenate([a_hi, a_lo, a_hi], axis=1), jnp.concatenate([b_hi, b_hi, b_lo], axis=0))


def _sigmoid(x):
    return 1.0 / (1.0 + jnp.exp(-x))


def _silu(x):
    return x * _sigmoid(x)


def _softplus(x):
    return jnp.maximum(x, 0.0) + jnp.log1p(jnp.exp(-jnp.abs(x)))


def _rms_rows(x, w):
    return x * lax.rsqrt(jnp.mean(x * x, axis=-1, keepdims=True) + EPS) * w


def _tri_masks(d):
    r = lax.broadcasted_iota(jnp.int32, (CHUNK, CHUNK), 0)
    c = lax.broadcasted_iota(jnp.int32, (CHUNK, CHUNK), 1)
    lag = (r - c) * (1 - 2 * d)
    return lag >= 0, lag > 0


def _adaln_kernel(c_ref, w_ref, b_ref, o_ref):
    c = c_ref[...]
    o_ref[0] = jnp.dot(_silu(c), w_ref[0], preferred_element_type=F32,
                       precision=lax.Precision.HIGHEST) + b_ref[0]


def _adaln(c_pad, adaln_w, adaln_b):
    depth, d, n = adaln_w.shape
    tn = 1536
    return pl.pallas_call(
        _adaln_kernel,
        grid=(depth, n // tn),
        in_specs=[pl.BlockSpec((8, d), lambda l, j: (0, 0)),
                  pl.BlockSpec((1, d, tn), lambda l, j: (l, 0, j)),
                  pl.BlockSpec((1, 1, tn), lambda l, j: (l, 0, j))],
        out_specs=pl.BlockSpec((1, 8, tn), lambda l, j: (l, 0, j)),
        out_shape=jax.ShapeDtypeStruct((depth, 8, n), F32),
        compiler_params=_cparams(("parallel", "parallel")),
        name="adaln",
    )(c_pad, adaln_w, adaln_b.reshape(depth, 1, n))


def _inproj_kernel(x_ref, nw_ref, sc_ref, sh_ref, w_ref, o_ref, h_ref):
    @pl.when(pl.program_id(1) == 0)
    def _():
        y = _rms_rows(x_ref[...], nw_ref[...])
        h_ref[...] = (y * (1.0 + sc_ref[0]) + sh_ref[0]).astype(BF16)

    o_ref[...] = _dot(h_ref[...], w_ref[...]).astype(o_ref.dtype)


def _inproj(x2, nw, sc, sh, w, seq):
    m, d = x2.shape
    n = w.shape[1]
    tm, tn = 1024, 768
    per_b = seq // tm
    return pl.pallas_call(
        _inproj_kernel,
        grid=(m // tm, n // tn),
        in_specs=[pl.BlockSpec((tm, d), lambda i, j: (i, 0)),
                  pl.BlockSpec((1, d), lambda i, j: (0, 0)),
                  pl.BlockSpec((1, 1, d), lambda i, j: (i // per_b, 0, 0)),
                  pl.BlockSpec((1, 1, d), lambda i, j: (i // per_b, 0, 0)),
                  pl.BlockSpec((d, tn), lambda i, j: (0, j))],
        out_specs=pl.BlockSpec((tm, tn), lambda i, j: (i, j)),
        out_shape=jax.ShapeDtypeStruct((m, n), BF16),
        scratch_shapes=[pltpu.VMEM((tm, d), BF16)],
        compiler_params=_cparams(("parallel", "arbitrary")),
        name="inproj",
    )(x2, nw, sc, sh, w)


GLA_TL = 512
GLA_CB = 8


def _gla_kernel(q_ref, k_ref, v_ref, sm_ref, up_ref, bias_ref, o_ref, st_ref):
    d = pl.program_id(0)
    nc = GLA_TL // CHUNK

    @pl.when(pl.program_id(2) == 0)
    def _():
        st_ref[...] = jnp.zeros_like(st_ref)

    incl, _ = _tri_masks(d)
    tri = jnp.where(incl, 1.0, 0.0).astype(BF16)
    tri3 = jnp.concatenate([tri, tri, tri], axis=1)
    fwd = d == 0
    scale = GLA_DK ** -0.5

    H = GLA_HEADS
    ks = [slice(h * GLA_DK, (h + 1) * GLA_DK) for h in range(H)]
    vs = [slice(h * GLA_DV, (h + 1) * GLA_DV) for h in range(H)]

    def group(gi, carry):
        rows = []
        for ci in range(GLA_CB):
            c = gi * GLA_CB + ci
            ce = jnp.where(fwd, c, nc - 1 - c)
            rows.append(pl.ds(pl.multiple_of(ce * CHUNK, CHUNK), CHUNK))
        cr = range(GLA_CB)
        sm = [sm_ref[0, r, :] for r in rows]
        z = [_dot(jnp.concatenate([x, x], axis=1), up_ref[0]) + bias_ref[0] for x in sm]
        g = [(jnp.minimum(x, 0.0) - jnp.log1p(jnp.exp(-jnp.abs(x)))) * (1.0 / GLA_TAU) for x in z]
        b = [_dot(tri3, _split3_rows(x)) for x in g]
        b_last = [jnp.where(fwd, x[CHUNK - 1:CHUNK], x[0:1]) for x in b]
        b_mid = [jnp.where(fwd, x[CHUNK // 2 - 1:CHUNK // 2], x[CHUNK // 2:CHUNK // 2 + 1]) for x in b]
        q = [q_ref[0, r, :].astype(F32) * scale for r in rows]
        k = [k_ref[0, r, :].astype(F32) for r in rows]
        qe = [(q[i] * jnp.exp(b[i] - b_mid[i])).astype(BF16) for i in cr]
        ke = [(k[i] * jnp.exp(b_mid[i] - b[i])).astype(BF16) for i in cr]
        kend = [(k[i] * jnp.exp(b_last[i] - b[i])).astype(BF16) for i in cr]
        qd = [(q[i] * jnp.exp(b[i])).astype(BF16) for i in cr]
        dec = [jnp.exp(x) for x in b_last]
        ch = [(ci, h) for ci in cr for h in range(H)]
        v_a = [v_ref[0, rows[ci], vs[h]] for ci, h in ch]
        a = [jnp.where(incl, _dot_nt(qe[ci][:, ks[h]], ke[ci][:, ks[h]]), 0.0).astype(BF16) for ci, h in ch]
        o_in = [_dot(a[i], v_a[i]) for i in range(len(ch))]
        kv = [_dot_tn(v_a[i], kend[ci][:, ks[h]]) for i, (ci, h) in enumerate(ch)]
        st = st_ref[...]
        for ci in cr:
            st_b = st.astype(BF16)
            for h in range(H):
                o = o_in[ci * H + h] + _dot_nt(qd[ci][:, ks[h]], st_b[:, ks[h]])
                o_ref[0, 0, rows[ci], vs[h]] = o.astype(o_ref.dtype)
            st = st * dec[ci] + jnp.concatenate([kv[ci * H + h] for h in range(H)], axis=1)
        st_ref[...] = st
        return carry

    lax.fori_loop(0, nc // GLA_CB, group, 0)


def _gla(p3, up2, bias, batch, seq):
    nblk = seq // GLA_TL

    def blk(dd, j):
        return j + dd * (nblk - 1 - 2 * j)

    return pl.pallas_call(
        _gla_kernel,
        grid=(2, batch, nblk),
        in_specs=[pl.BlockSpec((1, GLA_TL, 256), lambda dd, b, j: (b, blk(dd, j), P_GQ // 256)),
                  pl.BlockSpec((1, GLA_TL, 256), lambda dd, b, j: (b, blk(dd, j), P_GK // 256)),
                  pl.BlockSpec((1, GLA_TL, 512), lambda dd, b, j: (b, blk(dd, j), P_GV // 512)),
                  pl.BlockSpec((1, GLA_TL, LANE), lambda dd, b, j: (b, blk(dd, j), P_SMALL // LANE)),
                  pl.BlockSpec((1, 2 * LANE, 256), lambda dd, b, j: (dd, 0, 0)),
                  pl.BlockSpec((1, 1, 256), lambda dd, b, j: (dd, 0, 0))],
        out_specs=pl.BlockSpec((1, 1, GLA_TL, 512), lambda dd, b, j: (dd, b, blk(dd, j), 0)),
        out_shape=jax.ShapeDtypeStruct((2, batch, seq, 512), BF16),
        scratch_shapes=[pltpu.VMEM((GLA_DV, GLA_HEADS * GLA_DK), F32)],
        compiler_params=_cparams(("parallel", "parallel", "arbitrary")),
        name="gla",
    )(p3, p3, p3, p3, up2, bias)


ROPE_TL = 512


def _rope_kernel(x_ref, pos_ref, inv_ref, o_ref):
    s = pl.program_id(2)
    ang = pos_ref[0].astype(F32) * inv_ref[...]
    cosf = jnp.cos(ang)
    sinf = jnp.sin(ang)
    lane = lax.broadcasted_iota(jnp.int32, (1, LANE), 1) % DIFF_DH
    half = ROT_DIM // 2
    cmul = jnp.where(lane < ROT_DIM, cosf, 1.0)
    s_up = jnp.where(lane < half, -sinf, 0.0)
    s_dn = jnp.where((lane >= half) & (lane < ROT_DIM), sinf, 0.0)
    scale = jnp.where(s == 0, DIFF_DH ** -0.5 * math.log2(math.e), 1.0)
    for gidx in range(4):
        cs = slice(gidx * LANE, (gidx + 1) * LANE)
        x = x_ref[0, :, cs].astype(F32)
        y = x * cmul + pltpu.roll(x, LANE - half, 1) * s_up + pltpu.roll(x, half, 1) * s_dn
        o_ref[0, 0, :, cs] = (y * scale).astype(o_ref.dtype)


def _rope(p3, pos3, inv_row, batch, seq):
    nblk = seq // ROPE_TL
    return pl.pallas_call(
        _rope_kernel,
        grid=(batch, nblk, 2),
        in_specs=[pl.BlockSpec((1, ROPE_TL, 512), lambda b, j, s: (b, j, s)),
                  pl.BlockSpec((1, ROPE_TL, 1), lambda b, j, s: (b, j, 0)),
                  pl.BlockSpec((1, LANE), lambda b, j, s: (0, 0))],
        out_specs=pl.BlockSpec((1, 1, ROPE_TL, 512), lambda b, j, s: (s, b, j, 0)),
        out_shape=jax.ShapeDtypeStruct((2, batch, seq, 512), BF16),
        compiler_params=_cparams(("parallel", "parallel", "arbitrary")),
        name="rope",
    )(p3, pos3, inv_row)


ATT_TQ = 512
ATT_TK = 2048


def _attn_kernel(q_ref, k_ref, v_ref, lam_ref, o_ref, sa_ref, sb_ref, *, lam_init):
    q = q_ref[0, 0]
    lane = lax.broadcasted_iota(jnp.int32, q.shape, 1)
    zero = jnp.zeros_like(q)
    q1 = jnp.where(lane < DIFF_DH, q, zero)
    q2 = jnp.where(lane >= DIFF_DH, q, zero)
    tq = q.shape[0]
    tk = sa_ref.shape[2]
    nk = k_ref.shape[2] // tk
    ones_col = jnp.where(lax.broadcasted_iota(jnp.int32, (tk, LANE), 1) == 0, 1.0, 0.0).astype(BF16)

    def soft(s, v1, m, l, acc):
        m_new = jnp.maximum(m, jnp.max(s, axis=-1, keepdims=True))
        p = jnp.exp2((s - m_new).astype(BF16))
        alpha = jnp.exp2(m - m_new)
        pv = _dot(p, v1)
        return m_new, alpha * l + pv[:, LANE:LANE + 1], alpha * acc + pv[:, :LANE]

    def k_tile(t):
        return k_ref[0, 0, pl.ds(pl.multiple_of(t * tk, tk), tk), :]

    def v_tile(t):
        return jnp.concatenate([v_ref[0, pl.ds(pl.multiple_of(t * tk, tk), tk), :], ones_col], axis=1)

    def scores(t, dst):
        k = k_tile(t)
        dst[0] = _dot_nt(q1, k)
        dst[1] = _dot_nt(q2, k)

    def consume(t, src, carry):
        m1, l1, a1, m2, l2, a2 = carry
        v1 = v_tile(t)
        m1, l1, a1 = soft(src[0], v1, m1, l1, a1)
        m2, l2, a2 = soft(src[1], v1, m2, l2, a2)
        return m1, l1, a1, m2, l2, a2

    scores(0, sa_ref)

    def body(u, carry):
        t = 2 * u
        scores(t + 1, sb_ref)
        carry = consume(t, sa_ref, carry)
        scores(jnp.minimum(t + 2, nk - 1), sa_ref)
        return consume(t + 1, sb_ref, carry)

    neg = jnp.full((tq, 1), -jnp.inf, F32)
    zl = jnp.zeros((tq, 1), F32)
    za = jnp.zeros((tq, LANE), F32)
    m1, l1, a1, m2, l2, a2 = lax.fori_loop(0, nk // 2, body, (neg, zl, za, neg, zl, za))
    lp = lam_ref[...]
    lam = (jnp.exp(jnp.sum(lp[0:1] * lp[1:2], axis=-1, keepdims=True))
           - jnp.exp(jnp.sum(lp[2:3] * lp[3:4], axis=-1, keepdims=True)) + lam_init)
    o_ref[0] = (a1 / l1 - lam * (a2 / l2)).astype(o_ref.dtype)


def _attn(qk, p3, lam_p, lam_init, batch, seq):
    nq = seq // ATT_TQ
    tk = min(ATT_TK, seq // 2)
    return pl.pallas_call(
        functools.partial(_attn_kernel, lam_init=lam_init),
        grid=(batch, DIFF_HEADS, nq),
        in_specs=[pl.BlockSpec((1, 1, ATT_TQ, LANE), lambda b, h, i: (0, b, i, h)),
                  pl.BlockSpec((1, 1, seq, LANE), lambda b, h, i: (1, b, 0, h)),
                  pl.BlockSpec((1, seq, LANE), lambda b, h, i: (b, 0, P_DV // LANE + h)),
                  pl.BlockSpec((4, DIFF_DH), lambda b, h, i: (0, 0))],
        out_specs=pl.BlockSpec((1, ATT_TQ, LANE), lambda b, h, i: (b, i, h)),
        out_shape=jax.ShapeDtypeStruct((batch, seq, 512), BF16),
        scratch_shapes=[pltpu.VMEM((2, ATT_TQ, tk), F32), pltpu.VMEM((2, ATT_TQ, tk), F32)],
        compiler_params=_cparams(("parallel", "parallel", "arbitrary")),
        name="diffattn",
    )(qk, qk, p3, lam_p)


GPREP_TL = 512
HALO = 16


def _gdn_prep_kernel(x_ref, prev_ref, next_ref, w_ref, o_ref, xe_ref):
    j = pl.program_id(1)
    s = pl.program_id(2)
    nblk = pl.num_programs(1)
    tl = GPREP_TL
    prev = prev_ref[0].astype(F32)
    nxt = next_ref[0].astype(F32)
    xe_ref[0:HALO, :] = jnp.where(j == 0, 0.0, prev)
    xe_ref[HALO:HALO + tl, :] = x_ref[0].astype(F32)
    xe_ref[HALO + tl:2 * HALO + tl, :] = jnp.where(j == nblk - 1, 0.0, nxt)
    w = w_ref[...]
    y = (xe_ref[pl.ds(HALO - 1, tl), :] * w[0:1] + xe_ref[pl.ds(HALO, tl), :] * w[1:2]
         + xe_ref[pl.ds(HALO + 1, tl), :] * w[2:3] + xe_ref[pl.ds(HALO + 2, tl), :] * w[3:4])
    y = _silu(y)
    qscale = jnp.where(s == 0, GDN_DK ** -0.5, 1.0)
    for h in range(GDN_HEADS):
        cs = slice(h * GDN_DK, (h + 1) * GDN_DK)
        yh = y[:, cs]
        nrm = lax.rsqrt(jnp.sum(yh * yh, axis=-1, keepdims=True) + EPS) * qscale
        fac = jnp.where(s == 2, 1.0, nrm)
        o_ref[0, 0, :, cs] = (yh * fac).astype(o_ref.dtype)


def _gdn_prep(p3, conv_w, batch, seq):
    nblk = seq // GPREP_TL
    hb = GPREP_TL // HALO
    last = seq // HALO - 1
    base = P_NQ // 512
    return pl.pallas_call(
        _gdn_prep_kernel,
        grid=(batch, nblk, 3),
        in_specs=[pl.BlockSpec((1, GPREP_TL, 512), lambda b, j, s: (b, j, base + s)),
                  pl.BlockSpec((1, HALO, 512), lambda b, j, s: (b, jnp.maximum(j * hb - 1, 0), base + s)),
                  pl.BlockSpec((1, HALO, 512), lambda b, j, s: (b, jnp.minimum((j + 1) * hb, last), base + s)),
                  pl.BlockSpec((CONV_W, 512), lambda b, j, s: (0, s))],
        out_specs=pl.BlockSpec((1, 1, GPREP_TL, 512), lambda b, j, s: (s, b, j, 0)),
        out_shape=jax.ShapeDtypeStruct((3, batch, seq, 512), BF16),
        scratch_shapes=[pltpu.VMEM((GPREP_TL + 2 * HALO, 512), F32)],
        compiler_params=_cparams(("parallel", "parallel", "arbitrary")),
        name="gdn_prep",
    )(p3, p3, p3, conv_w)


GDN_TL = 512
GDN_CB = 4


def _gdn_kernel(q_ref, k_ref, v_ref, sm_ref, sel_ref, nega_ref, dtb_ref, o_ref, s_ref):
    d = pl.program_id(0)
    nc = GDN_TL // CHUNK

    @pl.when(pl.program_id(2) == 0)
    def _():
        s_ref[...] = jnp.zeros_like(s_ref)

    incl, strict = _tri_masks(d)
    tri = jnp.where(incl, 1.0, 0.0).astype(BF16)
    tri3 = jnp.concatenate([tri, tri, tri], axis=1)
    upper = jnp.where(strict, 1.0, 0.0)
    r_i = lax.broadcasted_iota(jnp.int32, (CHUNK, CHUNK), 0)
    c_i = lax.broadcasted_iota(jnp.int32, (CHUNK, CHUNK), 1)
    eye = jnp.where(r_i == c_i, 1.0, 0.0)
    fwd = d == 0
    H = GDN_HEADS

    def group(gi, carry):
        rows = []
        for ci in range(GDN_CB):
            c = gi * GDN_CB + ci
            ce = jnp.where(fwd, c, nc - 1 - c)
            rows.append(pl.ds(pl.multiple_of(ce * CHUNK, CHUNK), CHUNK))
        sel = [_dot(sm_ref[0, r, :], sel_ref[0]) for r in rows]
        beta = [_sigmoid(x) for x in sel]
        g = [nega_ref[0] * _softplus(x + dtb_ref[0]) for x in sel]
        g2 = [jnp.concatenate([x[:, H + h:H + h + 1] * upper for h in range(H)] + [x], axis=1) for x in g]
        dd = [_dot(tri3, _split3_rows(x)) for x in g2]
        ch = [(ci, h) for ci in range(GDN_CB) for h in range(H)]
        hs = [slice(h * GDN_DK, (h + 1) * GDN_DK) for h in range(H)]
        k_a = [k_ref[0, 0, rows[ci], hs[h]] for ci, h in ch]
        q_a = [q_ref[0, 0, rows[ci], hs[h]] for ci, h in ch]
        v_a = [v_ref[0, 0, rows[ci], hs[h]] for ci, h in ch]
        n_ch = range(len(ch))
        kf = [k_a[i].astype(F32) for i in n_ch]
        gam = [jnp.where(incl, jnp.exp(dd[ci][:, h * CHUNK:(h + 1) * CHUNK]), 0.0) for ci, h in ch]
        bcol = [dd[ci][:, H * CHUNK + H + h:H * CHUNK + H + h + 1] for ci, h in ch]
        blast = [jnp.where(fwd, b[CHUNK - 1:CHUNK], b[0:1]) for b in bcol]
        eb = [jnp.exp(b) for b in bcol]
        bet = [beta[ci][:, h:h + 1] for ci, h in ch]
        kk = [_dot_nt(k_a[i], k_a[i]) for i in n_ch]
        qk = [_dot_nt(q_a[i], k_a[i]) for i in n_ch]
        pw = [jnp.where(strict, -(bet[i] * kk[i] * gam[i]), 0.0) for i in n_ch]
        t = [eye + pw[i] for i in n_ch]
        for _ in range(5):
            pw = [_dot_split(pw[i], pw[i]) for i in n_ch]
            t = [t[i] + _dot_split(t[i], pw[i]) for i in n_ch]
        rhs = [jnp.concatenate([v_a[i].astype(F32) * bet[i], kf[i] * (bet[i] * eb[i])], axis=1).astype(BF16)
               for i in n_ch]
        sol = [_dot(t[i].astype(BF16), rhs[i]) for i in n_ch]
        u = [sol[i][:, :GDN_DV] for i in n_ch]
        w = [sol[i][:, GDN_DV:].astype(BF16) for i in n_ch]
        attn = [(qk[i] * gam[i]).astype(BF16) for i in n_ch]
        qdec = [(q_a[i].astype(F32) * eb[i]).astype(BF16) for i in n_ch]
        kend = [(kf[i] * jnp.exp(blast[i] - bcol[i])).astype(BF16) for i in n_ch]
        dec = [jnp.exp(b) for b in blast]
        st = [s_ref[h] for h in range(H)]
        for ci in range(GDN_CB):
            ix = [ci * H + h for h in range(H)]
            st_b = [st[h].astype(BF16) for h in range(H)]
            vnb = [(u[i] - _dot(w[i], st_b[h])).astype(BF16) for h, i in enumerate(ix)]
            o_new = [(_dot(qdec[i], st_b[h]) + _dot(attn[i], vnb[h])).astype(o_ref.dtype) for h, i in enumerate(ix)]
            st = [dec[i] * st[h] + _dot_tn(kend[i], vnb[h]) for h, i in enumerate(ix)]
            for h in range(H):
                o_ref[0, 0, rows[ci], hs[h]] = o_new[h]
        for h in range(H):
            s_ref[h] = st[h]
        return carry

    lax.fori_loop(0, nc // GDN_CB, group, 0)


def _gdn(qkv, p3, sel_m, nega, dtb, batch, seq):
    nblk = seq // GDN_TL

    def blk(dd, j):
        return j + dd * (nblk - 1 - 2 * j)

    def qspec(idx):
        return pl.BlockSpec((1, 1, GDN_TL, 512), lambda dd, b, j: (idx, b, blk(dd, j), 0))

    return pl.pallas_call(
        _gdn_kernel,
        grid=(2, batch, nblk),
        in_specs=[qspec(0), qspec(1), qspec(2),
                  pl.BlockSpec((1, GDN_TL, LANE), lambda dd, b, j: (b, blk(dd, j), P_SMALL // LANE)),
                  pl.BlockSpec((1, LANE, LANE), lambda dd, b, j: (dd, 0, 0)),
                  pl.BlockSpec((1, 1, LANE), lambda dd, b, j: (dd, 0, 0)),
                  pl.BlockSpec((1, 1, LANE), lambda dd, b, j: (dd, 0, 0))],
        out_specs=pl.BlockSpec((1, 1, GDN_TL, 512), lambda dd, b, j: (dd, b, blk(dd, j), 0)),
        out_shape=jax.ShapeDtypeStruct((2, batch, seq, 512), BF16),
        scratch_shapes=[pltpu.VMEM((GDN_HEADS, GDN_DK, GDN_DV), F32)],
        compiler_params=_cparams(("parallel", "parallel", "arbitrary")),
        name="gdn",
    )(qkv, qkv, qkv, p3, sel_m, nega, dtb)


MERGE_TM = 512


def _head_rms(o, w):
    parts = []
    for h in range(4):
        oh = o[:, h * LANE:(h + 1) * LANE]
        parts.append(oh * lax.rsqrt(jnp.mean(oh * oh, axis=-1, keepdims=True) + EPS))
    return jnp.concatenate(parts, axis=1) * w


def _merge_kernel(gf_ref, gb_ref, gr_ref, od_ref, nf_ref, nb_ref, nz_ref, mg0_ref, mg1_ref, mg2_ref,
                  x_ref, g1_ref, nrm_ref, wb_ref, wo_ref, nw_ref, o_ref, *, diff_scale):
    nrm = nrm_ref[...]
    y_gla = _head_rms(gf_ref[0, 0].astype(F32) + gb_ref[0, 0].astype(F32), nrm[0:1]) * _silu(gr_ref[0].astype(F32))
    y_diff = _head_rms(od_ref[0].astype(F32), nrm[1:2]) * diff_scale
    y_gdn = _head_rms(nf_ref[0, 0].astype(F32) + nb_ref[0, 0].astype(F32), nrm[2:3]) * _silu(nz_ref[0].astype(F32))
    merged = (_sigmoid(mg0_ref[0].astype(F32)) * _dot(y_gla.astype(BF16), wb_ref[0])
              + _sigmoid(mg1_ref[0].astype(F32)) * _dot(y_diff.astype(BF16), wb_ref[1])
              + _sigmoid(mg2_ref[0].astype(F32)) * _dot(y_gdn.astype(BF16), wb_ref[2]))
    y = _dot(merged.astype(BF16), wo_ref[...])
    o_ref[0] = x_ref[0] + g1_ref[0] * _rms_rows(y, nw_ref[...])


def _merge(gla_o, p3, att_o, gdn_o, x3, g1, nrm3, wb, wo, nw, diff_scale, batch, seq):
    tm = MERGE_TM
    nblk = seq // tm
    d = D_MODEL

    def dirspec(idx):
        return pl.BlockSpec((1, 1, tm, 512), lambda b, j: (idx, b, j, 0))

    def pspec(width, col):
        return pl.BlockSpec((1, tm, width), lambda b, j: (b, j, col // width))

    return pl.pallas_call(
        functools.partial(_merge_kernel, diff_scale=diff_scale),
        grid=(batch, nblk),
        in_specs=[dirspec(0), dirspec(1), pspec(512, P_GR),
                  pl.BlockSpec((1, tm, 512), lambda b, j: (b, j, 0)),
                  dirspec(0), dirspec(1), pspec(512, P_NZ),
                  pspec(d, P_MG), pspec(d, P_MG + d), pspec(d, P_MG + 2 * d),
                  pl.BlockSpec((1, tm, d), lambda b, j: (b, j, 0)),
                  pl.BlockSpec((1, 1, d), lambda b, j: (b, 0, 0)),
                  pl.BlockSpec((3, 512), lambda b, j: (0, 0)),
                  pl.BlockSpec((3, 512, d), lambda b, j: (0, 0, 0)),
                  pl.BlockSpec((d, d), lambda b, j: (0, 0)),
                  pl.BlockSpec((1, d), lambda b, j: (0, 0))],
        out_specs=pl.BlockSpec((1, tm, d), lambda b, j: (b, j, 0)),
        out_shape=jax.ShapeDtypeStruct((batch, seq, d), F32),
        compiler_params=_cparams(("parallel", "parallel")),
        name="merge",
    )(gla_o, gla_o, p3, att_o, gdn_o, gdn_o, p3, p3, p3, p3, x3, g1, nrm3, wb, wo, nw)


FFN_TM = 512


def _ffn_kernel(x_ref, nw_ref, sc_ref, sh_ref, w1_ref, w3_ref, w2_ref, g2_ref, nwo_ref, o_ref, h_ref, acc_ref):
    j = pl.program_id(1)

    @pl.when(j == 0)
    def _():
        y = _rms_rows(x_ref[...], nw_ref[...])
        h_ref[...] = (y * (1.0 + sc_ref[0]) + sh_ref[0]).astype(BF16)
        acc_ref[...] = jnp.zeros_like(acc_ref)

    h = h_ref[...]
    a = _dot(h, w1_ref[...])
    b = _dot(h, w3_ref[...])
    acc_ref[...] += _dot((_silu(a) * b).astype(BF16), w2_ref[...])

    @pl.when(j == pl.num_programs(1) - 1)
    def _():
        o_ref[...] = x_ref[...] + g2_ref[0] * _rms_rows(acc_ref[...], nwo_ref[...])


def _ffn(x2, nw, sc, sh, w1, w3, w2, g2, nwo, seq):
    m, d = x2.shape
    f = w1.shape[1]
    tm, tf = FFN_TM, f // 2
    per_b = seq // tm
    vec = pl.BlockSpec((1, d), lambda i, j: (0, 0))
    mod = pl.BlockSpec((1, 1, d), lambda i, j: (i // per_b, 0, 0))
    return pl.pallas_call(
        _ffn_kernel,
        grid=(m // tm, f // tf),
        in_specs=[pl.BlockSpec((tm, d), lambda i, j: (i, 0)), vec, mod, mod,
                  pl.BlockSpec((d, tf), lambda i, j: (0, j)),
                  pl.BlockSpec((d, tf), lambda i, j: (0, j)),
                  pl.BlockSpec((tf, d), lambda i, j: (j, 0)),
                  mod, vec],
        out_specs=pl.BlockSpec((tm, d), lambda i, j: (i, 0)),
        out_shape=jax.ShapeDtypeStruct((m, d), F32),
        scratch_shapes=[pltpu.VMEM((tm, d), BF16), pltpu.VMEM((tm, d), F32)],
        compiler_params=_cparams(("parallel", "arbitrary")),
        name="ffn",
    )(x2, nw, sc, sh, w1, w3, w2, g2, nwo)


ROUTER_TM = 1024


def _router_kernel(x_ref, nw_ref, sc_ref, sh_ref, rw_ref, o_ref):
    y = _rms_rows(x_ref[...], nw_ref[...])
    h = y * (1.0 + sc_ref[0]) + sh_ref[0]
    logits = jnp.dot(h, rw_ref[...], preferred_element_type=F32, precision=lax.Precision.HIGHEST)
    lane = lax.broadcasted_iota(jnp.int32, logits.shape, 1)
    neg = -jnp.inf
    logits = jnp.where(lane < N_EXPERTS, logits, neg)
    m1 = jnp.max(logits, axis=-1, keepdims=True)
    i1 = jnp.min(jnp.where(logits == m1, lane, LANE), axis=-1, keepdims=True)
    rest = jnp.where(lane == i1, neg, logits)
    m2 = jnp.max(rest, axis=-1, keepdims=True)
    i2 = jnp.min(jnp.where(rest == m2, lane, LANE), axis=-1, keepdims=True)
    e = jnp.exp(m2 - m1)
    w1 = 1.0 / (1.0 + e)
    w2 = e / (1.0 + e)
    o_ref[...] = jnp.where(lane == i1, w1, 0.0) + jnp.where(lane == i2, w2, 0.0)


def _router(x2, nw, sc, sh, rw_pad, seq):
    m, d = x2.shape
    tm = ROUTER_TM
    per_b = seq // tm
    return pl.pallas_call(
        _router_kernel,
        grid=(m // tm,),
        in_specs=[pl.BlockSpec((tm, d), lambda i: (i, 0)),
                  pl.BlockSpec((1, d), lambda i: (0, 0)),
                  pl.BlockSpec((1, 1, d), lambda i: (i // per_b, 0, 0)),
                  pl.BlockSpec((1, 1, d), lambda i: (i // per_b, 0, 0)),
                  pl.BlockSpec((d, LANE), lambda i: (0, 0))],
        out_specs=pl.BlockSpec((tm, LANE), lambda i: (i, 0)),
        out_shape=jax.ShapeDtypeStruct((m, LANE), F32),
        compiler_params=_cparams(("parallel",)),
        name="router",
    )(x2, nw, sc, sh, rw_pad)


MOE_TM = 1024
MOE_TF = 512


def _moe_kernel(x_ref, nw_ref, sc_ref, sh_ref, cw_ref, w1_ref, w3_ref, w2_ref, g2_ref, nwo_ref, o_ref,
                h_ref, acc_ref):
    e = pl.program_id(1)
    j = pl.program_id(2)
    first = (e == 0) & (j == 0)
    last = (e == pl.num_programs(1) - 1) & (j == pl.num_programs(2) - 1)

    @pl.when(first)
    def _():
        y = _rms_rows(x_ref[...], nw_ref[...])
        h_ref[...] = (y * (1.0 + sc_ref[0]) + sh_ref[0]).astype(BF16)
        acc_ref[...] = jnp.zeros_like(acc_ref)

    cw = cw_ref[...]
    lane = lax.broadcasted_iota(jnp.int32, cw.shape, 1)
    ce = jnp.sum(jnp.where(lane == e, cw, 0.0), axis=-1, keepdims=True)
    h = h_ref[...]
    a = _dot(h, w1_ref[0])
    b = _dot(h, w3_ref[0])
    acc_ref[...] += _dot((_silu(a) * b * ce).astype(BF16), w2_ref[0])

    @pl.when(last)
    def _():
        o_ref[...] = x_ref[...] + g2_ref[0] * _rms_rows(acc_ref[...], nwo_ref[...])


def _moe(x2, nw, sc, sh, cw, w1, w3, w2, g2, nwo, seq):
    m, d = x2.shape
    ne, _, f = w1.shape
    tm, tf = MOE_TM, MOE_TF
    per_b = seq // tm
    vec = pl.BlockSpec((1, d), lambda i, e, j: (0, 0))
    mod = pl.BlockSpec((1, 1, d), lambda i, e, j: (i // per_b, 0, 0))
    return pl.pallas_call(
        _moe_kernel,
        grid=(m // tm, ne, f // tf),
        in_specs=[pl.BlockSpec((tm, d), lambda i, e, j: (i, 0)), vec, mod, mod,
                  pl.BlockSpec((tm, LANE), lambda i, e, j: (i, 0)),
                  pl.BlockSpec((1, d, tf), lambda i, e, j: (e, 0, j)),
                  pl.BlockSpec((1, d, tf), lambda i, e, j: (e, 0, j)),
                  pl.BlockSpec((1, tf, d), lambda i, e, j: (e, j, 0)),
                  mod, vec],
        out_specs=pl.BlockSpec((tm, d), lambda i, e, j: (i, 0)),
        out_shape=jax.ShapeDtypeStruct((m, d), F32),
        scratch_shapes=[pltpu.VMEM((tm, d), BF16), pltpu.VMEM((tm, d), F32)],
        compiler_params=_cparams(("parallel", "arbitrary", "arbitrary")),
        name="moe",
    )(x2, nw, sc, sh, cw, w1, w3, w2, g2, nwo)


def _reorder_w_in(w):
    o = _IN_OFF

    def cols(a, b):
        return w[:, o[a]:o[b]]

    small = jnp.concatenate([cols(4, 6), cols(13, 17)], axis=1)
    pad = jnp.zeros((w.shape[0], P_COLS - P_SMALL - small.shape[1]), w.dtype)
    out = jnp.concatenate([cols(6, 9),
                           cols(2, 4),
                           cols(0, 2),
                           cols(9, 13),
                           cols(17, 18),
                           small, pad], axis=1)
    return out.astype(BF16)


def _gla_gate_params(up, bias):
    hi = up.astype(BF16)
    lo = (up - hi.astype(F32)).astype(BF16)
    out = jnp.zeros((2, 2 * LANE, GLA_HEADS * GLA_DK), BF16)
    for dd, off in ((0, S_GWF), (1, S_GWB)):
        out = out.at[dd, off:off + GLA_RANK].set(hi[dd])
        out = out.at[dd, LANE + off:LANE + off + GLA_RANK].set(lo[dd])
    return out, bias.reshape(2, 1, -1).astype(F32)


def _gdn_gate_params(a_log, dt_bias):
    h = GDN_HEADS
    sel = np.zeros((2, LANE, LANE), np.float32)
    for dd, (boff, aoff) in enumerate(((S_NBF, S_NAF), (S_NBB, S_NAB))):
        for i in range(h):
            sel[dd, boff + i, i] = 1.0
            sel[dd, aoff + i, h + i] = 1.0
    nega = jnp.zeros((2, 1, LANE), F32).at[:, 0, h:2 * h].set(-jnp.exp(a_log.astype(F32)))
    dtb = jnp.zeros((2, 1, LANE), F32).at[:, 0, h:2 * h].set(dt_bias.astype(F32))
    return jnp.asarray(sel, BF16), nega, dtb


def _rope_inv_row():
    inv = ROPE_THETA ** (-jnp.arange(0, ROT_DIM, 2, dtype=F32) / ROT_DIM)
    lane = np.arange(LANE) % DIFF_DH
    row = jnp.where(jnp.asarray(lane < ROT_DIM), inv[jnp.asarray(lane % (ROT_DIM // 2))], 0.0)
    return row.reshape(1, LANE).astype(F32)


def kernel(x, c, positions, adaln_w, adaln_b, norm_w, w_in, gla_gate_up, gla_gate_bias, gla_norm, diff_lambda,
           diff_norm, gdn_conv, gdn_A_log, gdn_dt_bias, gdn_norm, w_branch, w_out, ffn_w1, ffn_w3, ffn_w2,
           router_w, moe_w1, moe_w3, moe_w2):
    batch, seq, d = x.shape
    depth = adaln_w.shape[0]
    m = batch * seq

    c_pad = jnp.zeros((8, d), F32).at[:batch].set(c)
    mod = _adaln(c_pad, adaln_w, adaln_b)[:, :batch]
    pos3 = positions.reshape(batch, seq, 1)
    inv_row = _rope_inv_row()

    for layer in range(depth):
        sh1, sc1, g1, sh2, sc2, g2 = [t.reshape(batch, 1, d) for t in jnp.split(mod[layer], 6, axis=-1)]
        nw = norm_w[layer].astype(F32)

        p = _inproj(x.reshape(m, d), nw[0:1], sc1, sh1, _reorder_w_in(w_in[layer]), seq)
        p3 = p.reshape(batch, seq, P_COLS)

        up2, gbias = _gla_gate_params(gla_gate_up[layer], gla_gate_bias[layer])
        gla_o = _gla(p3, up2, gbias, batch, seq)

        qk = _rope(p3, pos3, inv_row, batch, seq)
        lam_init = 0.8 - 0.6 * math.exp(-0.3 * layer)
        att_o = _attn(qk, p3, diff_lambda[layer].astype(F32), lam_init, batch, seq)

        qkv = _gdn_prep(p3, gdn_conv[layer].astype(F32), batch, seq)
        sel_m, nega, dtb = _gdn_gate_params(gdn_A_log[layer], gdn_dt_bias[layer])
        gdn_o = _gdn(qkv, p3, sel_m, nega, dtb, batch, seq)

        nrm3 = jnp.stack([jnp.tile(gla_norm[layer], 4), jnp.tile(diff_norm[layer], 4),
                          jnp.tile(gdn_norm[layer], 4)]).astype(F32)
        x = _merge(gla_o, p3, att_o, gdn_o, x, g1, nrm3, w_branch[layer].astype(BF16),
                   w_out[layer].astype(BF16), nw[1:2], 1.0 - lam_init, batch, seq)

        x2 = x.reshape(m, d)
        if layer % 2 == 0:
            li = layer // 2
            x2 = _ffn(x2, nw[2:3], sc2, sh2, ffn_w1[li].astype(BF16), ffn_w3[li].astype(BF16),
                      ffn_w2[li].astype(BF16), g2, nw[3:4], seq)
        else:
            li = layer // 2
            rw_pad = jnp.zeros((d, LANE), F32).at[:, :N_EXPERTS].set(router_w[li])
            cw = _router(x2, nw[2:3], sc2, sh2, rw_pad, seq)
            x2 = _moe(x2, nw[2:3], sc2, sh2, cw, moe_w1[li].astype(BF16), moe_w3[li].astype(BF16),
                      moe_w2[li].astype(BF16), g2, nw[3:4], seq)
        x = x2.reshape(batch, seq, d)
    return x
```

```python
import functools
import math

import jax
import jax.numpy as jnp
import numpy as np
from jax import lax
from jax.experimental import pallas as pl
from jax.experimental.pallas import tpu as pltpu

F32 = jnp.float32
BF16 = jnp.bfloat16

D_MODEL = 1024
GLA_HEADS, GLA_DK, GLA_DV, GLA_RANK, GLA_TAU = 4, 64, 128, 16, 16.0
DIFF_HEADS, DIFF_DH = 4, 64
ROT_DIM = DIFF_DH // 4
ROPE_THETA = 500000.0
GDN_HEADS, GDN_DK, GDN_DV, CONV_W = 4, 128, 128, 4
CHUNK = 64
N_BRANCH = 3
N_EXPERTS, TOP_K = 8, 2
EPS = 1e-6
BRANCH_W = 512

_IN_SIZES = (256, 256, 512, 512, 16, 16, 512, 512, 512, 512, 512, 512, 512, 4, 4, 4, 4, 3 * D_MODEL)
_IN_OFF = tuple(int(v) for v in np.concatenate([[0], np.cumsum(_IN_SIZES)]))

P_DQ, P_DK, P_DV = 0, 512, 1024
P_GV, P_GR = 1536, 2048
P_GQ, P_GK = 2560, 2816
P_NQ, P_NK, P_NV, P_NZ = 3072, 3584, 4096, 4608
P_MG = 5120
P_SMALL = 8192
P_COLS = 8448
S_GWF, S_GWB, S_NBF, S_NBB, S_NAF, S_NAB = 0, 16, 32, 36, 40, 44

VMEM_LIMIT = 56 * 1024 * 1024
LANE = 128


def _cparams(sem):
    return pltpu.CompilerParams(dimension_semantics=sem, vmem_limit_bytes=VMEM_LIMIT)


def _dot(a, b):
    return jnp.dot(a, b, preferred_element_type=F32)


def _dot_nt(a, b):
    return lax.dot_general(a, b, (((1,), (1,)), ((), ())), preferred_element_type=F32)


def _dot_tn(a, b):
    return lax.dot_general(a, b, (((0,), (0,)), ((), ())), preferred_element_type=F32)


def _split3_rows(x):
    hi = x.astype(BF16)
    r1 = x - hi.astype(F32)
    mid = r1.astype(BF16)
    lo = (r1 - mid.astype(F32)).astype(BF16)
    return jnp.concatenate([hi, mid, lo], axis=0)


def _dot_split(a, b):
    a_hi = a.astype(BF16)
    a_lo = (a - a_hi.astype(F32)).astype(BF16)
    b_hi = b.astype(BF16)
    b_lo = (b - b_hi.astype(F32)).astype(BF16)
    return _dot(jnp.concatenate([a_hi, a_lo, a_hi], axis=1), jnp.concatenate([b_hi, b_hi, b_lo], axis=0))


def _sigmoid(x):
    return 1.0 / (1.0 + jnp.exp(-x))


def _silu(x):
    return x * _sigmoid(x)


def _softplus(x):
    return jnp.maximum(x, 0.0) + jnp.log1p(jnp.exp(-jnp.abs(x)))


def _rms_rows(x, w):
    return x * lax.rsqrt(jnp.mean(x * x, axis=-1, keepdims=True) + EPS) * w


def _tri_masks(d):
    r = lax.broadcasted_iota(jnp.int32, (CHUNK, CHUNK), 0)
    c = lax.broadcasted_iota(jnp.int32, (CHUNK, CHUNK), 1)
    lag = (r - c) * (1 - 2 * d)
    return lag >= 0, lag > 0


def _adaln_kernel(c_ref, w_ref, b_ref, o_ref):
    c = c_ref[...]
    o_ref[0] = jnp.dot(_silu(c), w_ref[0], preferred_element_type=F32,
                       precision=lax.Precision.HIGHEST) + b_ref[0]


def _adaln(c_pad, adaln_w, adaln_b):
    depth, d, n = adaln_w.shape
    tn = 1536
    return pl.pallas_call(
        _adaln_kernel,
        grid=(depth, n // tn),
        in_specs=[pl.BlockSpec((8, d), lambda l, j: (0, 0)),
                  pl.BlockSpec((1, d, tn), lambda l, j: (l, 0, j)),
                  pl.BlockSpec((1, 1, tn), lambda l, j: (l, 0, j))],
        out_specs=pl.BlockSpec((1, 8, tn), lambda l, j: (l, 0, j)),
        out_shape=jax.ShapeDtypeStruct((depth, 8, n), F32),
        compiler_params=_cparams(("parallel", "parallel")),
        name="adaln",
    )(c_pad, adaln_w, adaln_b.reshape(depth, 1, n))


def _inproj_kernel(x_ref, nw_ref, sc_ref, sh_ref, w_ref, o_ref, h_ref):
    @pl.when(pl.program_id(1) == 0)
    def _():
        y = _rms_rows(x_ref[...], nw_ref[...])
        h_ref[...] = (y * (1.0 + sc_ref[0]) + sh_ref[0]).astype(BF16)

    o_ref[...] = _dot(h_ref[...], w_ref[...]).astype(o_ref.dtype)


def _inproj(x2, nw, sc, sh, w, seq):
    m, d = x2.shape
    n = w.shape[1]
    tm, tn = 1024, 768
    per_b = seq // tm
    return pl.pallas_call(
        _inproj_kernel,
        grid=(m // tm, n // tn),
        in_specs=[pl.BlockSpec((tm, d), lambda i, j: (i, 0)),
                  pl.BlockSpec((1, d), lambda i, j: (0, 0)),
                  pl.BlockSpec((1, 1, d), lambda i, j: (i // per_b, 0, 0)),
                  pl.BlockSpec((1, 1, d), lambda i, j: (i // per_b, 0, 0)),
                  pl.BlockSpec((d, tn), lambda i, j: (0, j))],
        out_specs=pl.BlockSpec((tm, tn), lambda i, j: (i, j)),
        out_shape=jax.ShapeDtypeStruct((m, n), BF16),
        scratch_shapes=[pltpu.VMEM((tm, d), BF16)],
        compiler_params=_cparams(("parallel", "arbitrary")),
        name="inproj",
    )(x2, nw, sc, sh, w)


GLA_TL = 512
GLA_CB = 8


def _gla_kernel(q_ref, k_ref, v_ref, sm_ref, up_ref, bias_ref, o_ref, st_ref):
    d = pl.program_id(0)
    nc = GLA_TL // CHUNK

    @pl.when(pl.program_id(2) == 0)
    def _():
        st_ref[...] = jnp.zeros_like(st_ref)

    incl, _ = _tri_masks(d)
    tri = jnp.where(incl, 1.0, 0.0).astype(BF16)
    tri3 = jnp.concatenate([tri, tri, tri], axis=1)
    fwd = d == 0
    scale = GLA_DK ** -0.5

    H = GLA_HEADS
    ks = [slice(h * GLA_DK, (h + 1) * GLA_DK) for h in range(H)]
    vs = [slice(h * GLA_DV, (h + 1) * GLA_DV) for h in range(H)]

    def group(gi, carry):
        rows = []
        for ci in range(GLA_CB):
            c = gi * GLA_CB + ci
            ce = jnp.where(fwd, c, nc - 1 - c)
            rows.append(pl.ds(pl.multiple_of(ce * CHUNK, CHUNK), CHUNK))
        cr = range(GLA_CB)
        sm = [sm_ref[0, r, :] for r in rows]
        z = [_dot(jnp.concatenate([x, x], axis=1), up_ref[0]) + bias_ref[0] for x in sm]
        g = [(jnp.minimum(x, 0.0) - jnp.log1p(jnp.exp(-jnp.abs(x)))) * (1.0 / GLA_TAU) for x in z]
        b = [_dot(tri3, _split3_rows(x)) for x in g]
        b_last = [jnp.where(fwd, x[CHUNK - 1:CHUNK], x[0:1]) for x in b]
        b_mid = [jnp.where(fwd, x[CHUNK // 2 - 1:CHUNK // 2], x[CHUNK // 2:CHUNK // 2 + 1]) for x in b]
        q = [q_ref[0, r, :].astype(F32) * scale for r in rows]
        k = [k_ref[0, r, :].astype(F32) for r in rows]
        qe = [(q[i] * jnp.exp(b[i] - b_mid[i])).astype(BF16) for i in cr]
        ke = [(k[i] * jnp.exp(b_mid[i] - b[i])).astype(BF16) for i in cr]
        kend = [(k[i] * jnp.exp(b_last[i] - b[i])).astype(BF16) for i in cr]
        qd = [(q[i] * jnp.exp(b[i])).astype(BF16) for i in cr]
        dec = [jnp.exp(x) for x in b_last]
        ch = [(ci, h) for ci in cr for h in range(H)]
        v_a = [v_ref[0, rows[ci], vs[h]] for ci, h in ch]
        a = [jnp.where(incl, _dot_nt(qe[ci][:, ks[h]], ke[ci][:, ks[h]]), 0.0).astype(BF16) for ci, h in ch]
        o_in = [_dot(a[i], v_a[i]) for i in range(len(ch))]
        kv = [_dot_tn(v_a[i], kend[ci][:, ks[h]]) for i, (ci, h) in enumerate(ch)]
        st = st_ref[...]
        for ci in cr:
            st_b = st.astype(BF16)
            for h in range(H):
                o = o_in[ci * H + h] + _dot_nt(qd[ci][:, ks[h]], st_b[:, ks[h]])
                o_ref[0, 0, rows[ci], vs[h]] = o.astype(o_ref.dtype)
            st = st * dec[ci] + jnp.concatenate([kv[ci * H + h] for h in range(H)], axis=1)
        st_ref[...] = st
        return carry

    lax.fori_loop(0, nc // GLA_CB, group, 0)


def _gla(p3, up2, bias, batch, seq):
    nblk = seq // GLA_TL

    def blk(dd, j):
        return j + dd * (nblk - 1 - 2 * j)

    return pl.pallas_call(
        _gla_kernel,
        grid=(2, batch, nblk),
        in_specs=[pl.BlockSpec((1, GLA_TL, 256), lambda dd, b, j: (b, blk(dd, j), P_GQ // 256)),
                  pl.BlockSpec((1, GLA_TL, 256), lambda dd, b, j: (b, blk(dd, j), P_GK // 256)),
                  pl.BlockSpec((1, GLA_TL, 512), lambda dd, b, j: (b, blk(dd, j), P_GV // 512)),
                  pl.BlockSpec((1, GLA_TL, LANE), lambda dd, b, j: (b, blk(dd, j), P_SMALL // LANE)),
                  pl.BlockSpec((1, 2 * LANE, 256), lambda dd, b, j: (dd, 0, 0)),
                  pl.BlockSpec((1, 1, 256), lambda dd, b, j: (dd, 0, 0))],
        out_specs=pl.BlockSpec((1, 1, GLA_TL, 512), lambda dd, b, j: (dd, b, blk(dd, j), 0)),
        out_shape=jax.ShapeDtypeStruct((2, batch, seq, 512), BF16),
        scratch_shapes=[pltpu.VMEM((GLA_DV, GLA_HEADS * GLA_DK), F32)],
        compiler_params=_cparams(("parallel", "parallel", "arbitrary")),
        name="gla",
    )(p3, p3, p3, p3, up2, bias)


ROPE_TL = 512


def _rope_kernel(x_ref, pos_ref, inv_ref, o_ref):
    s = pl.program_id(2)
    ang = pos_ref[0].astype(F32) * inv_ref[...]
    cosf = jnp.cos(ang)
    sinf = jnp.sin(ang)
    lane = lax.broadcasted_iota(jnp.int32, (1, LANE), 1) % DIFF_DH
    half = ROT_DIM // 2
    cmul = jnp.where(lane < ROT_DIM, cosf, 1.0)
    s_up = jnp.where(lane < half, -sinf, 0.0)
    s_dn = jnp.where((lane >= half) & (lane < ROT_DIM), sinf, 0.0)
    scale = jnp.where(s == 0, DIFF_DH ** -0.5 * math.log2(math.e), 1.0)
    for gidx in range(4):
        cs = slice(gidx * LANE, (gidx + 1) * LANE)
        x = x_ref[0, :, cs].astype(F32)
        y = x * cmul + pltpu.roll(x, LANE - half, 1) * s_up + pltpu.roll(x, half, 1) * s_dn
        o_ref[0, 0, :, cs] = (y * scale).astype(o_ref.dtype)


def _rope(p3, pos3, inv_row, batch, seq):
    nblk = seq // ROPE_TL
    return pl.pallas_call(
        _rope_kernel,
        grid=(batch, nblk, 2),
        in_specs=[pl.BlockSpec((1, ROPE_TL, 512), lambda b, j, s: (b, j, s)),
                  pl.BlockSpec((1, ROPE_TL, 1), lambda b, j, s: (b, j, 0)),
                  pl.BlockSpec((1, LANE), lambda b, j, s: (0, 0))],
        out_specs=pl.BlockSpec((1, 1, ROPE_TL, 512), lambda b, j, s: (s, b, j, 0)),
        out_shape=jax.ShapeDtypeStruct((2, batch, seq, 512), BF16),
        compiler_params=_cparams(("parallel", "parallel", "arbitrary")),
        name="rope",
    )(p3, pos3, inv_row)


ATT_TQ = 512
ATT_TK = 2048


def _attn_kernel(q_ref, k_ref, v_ref, lam_ref, o_ref, sa_ref, sb_ref, *, lam_init):
    q = q_ref[0, 0]
    lane = lax.broadcasted_iota(jnp.int32, q.shape, 1)
    zero = jnp.zeros_like(q)
    q1 = jnp.where(lane < DIFF_DH, q, zero)
    q2 = jnp.where(lane >= DIFF_DH, q, zero)
    tq = q.shape[0]
    tk = sa_ref.shape[2]
    nk = k_ref.shape[2] // tk
    ones_col = jnp.where(lax.broadcasted_iota(jnp.int32, (tk, LANE), 1) == 0, 1.0, 0.0).astype(BF16)

    def soft(s, v1, m, l, acc):
        m_new = jnp.maximum(m, jnp.max(s, axis=-1, keepdims=True))
        p = jnp.exp2((s - m_new).astype(BF16))
        alpha = jnp.exp2(m - m_new)
        pv = _dot(p, v1)
        return m_new, alpha * l + pv[:, LANE:LANE + 1], alpha * acc + pv[:, :LANE]

    def k_tile(t):
        return k_ref[0, 0, pl.ds(pl.multiple_of(t * tk, tk), tk), :]

    def v_tile(t):
        return jnp.concatenate([v_ref[0, pl.ds(pl.multiple_of(t * tk, tk), tk), :], ones_col], axis=1)

    def scores(t, dst):
        k = k_tile(t)
        dst[0] = _dot_nt(q1, k)
        dst[1] = _dot_nt(q2, k)

    def consume(t, src, carry):
        m1, l1, a1, m2, l2, a2 = carry
        v1 = v_tile(t)
        m1, l1, a1 = soft(src[0], v1, m1, l1, a1)
        m2, l2, a2 = soft(src[1], v1, m2, l2, a2)
        return m1, l1, a1, m2, l2, a2

    scores(0, sa_ref)

    def body(u, carry):
        t = 2 * u
        scores(t + 1, sb_ref)
        carry = consume(t, sa_ref, carry)
        scores(jnp.minimum(t + 2, nk - 1), sa_ref)
        return consume(t + 1, sb_ref, carry)

    neg = jnp.full((tq, 1), -jnp.inf, F32)
    zl = jnp.zeros((tq, 1), F32)
    za = jnp.zeros((tq, LANE), F32)
    m1, l1, a1, m2, l2, a2 = lax.fori_loop(0, nk // 2, body, (neg, zl, za, neg, zl, za))
    lp = lam_ref[...]
    lam = (jnp.exp(jnp.sum(lp[0:1] * lp[1:2], axis=-1, keepdims=True))
           - jnp.exp(jnp.sum(lp[2:3] * lp[3:4], axis=-1, keepdims=True)) + lam_init)
    o_ref[0] = (a1 / l1 - lam * (a2 / l2)).astype(o_ref.dtype)


def _attn(qk, p3, lam_p, lam_init, batch, seq):
    nq = seq // ATT_TQ
    tk = min(ATT_TK, seq // 2)
    return pl.pallas_call(
        functools.partial(_attn_kernel, lam_init=lam_init),
        grid=(batch, DIFF_HEADS, nq),
        in_specs=[pl.BlockSpec((1, 1, ATT_TQ, LANE), lambda b, h, i: (0, b, i, h)),
                  pl.BlockSpec((1, 1, seq, LANE), lambda b, h, i: (1, b, 0, h)),
                  pl.BlockSpec((1, seq, LANE), lambda b, h, i: (b, 0, P_DV // LANE + h)),
                  pl.BlockSpec((4, DIFF_DH), lambda b, h, i: (0, 0))],
        out_specs=pl.BlockSpec((1, ATT_TQ, LANE), lambda b, h, i: (b, i, h)),
        out_shape=jax.ShapeDtypeStruct((batch, seq, 512), BF16),
        scratch_shapes=[pltpu.VMEM((2, ATT_TQ, tk), F32), pltpu.VMEM((2, ATT_TQ, tk), F32)],
        compiler_params=_cparams(("parallel", "parallel", "arbitrary")),
        name="diffattn",
    )(qk, qk, p3, lam_p)


GPREP_TL = 512
HALO = 16


def _gdn_prep_kernel(x_ref, prev_ref, next_ref, w_ref, o_ref, xe_ref):
    j = pl.program_id(1)
    s = pl.program_id(2)
    nblk = pl.num_programs(1)
    tl = GPREP_TL
    prev = prev_ref[0].astype(F32)
    nxt = next_ref[0].astype(F32)
    xe_ref[0:HALO, :] = jnp.where(j == 0, 0.0, prev)
    xe_ref[HALO:HALO + tl, :] = x_ref[0].astype(F32)
    xe_ref[HALO + tl:2 * HALO + tl, :] = jnp.where(j == nblk - 1, 0.0, nxt)
    w = w_ref[...]
    y = (xe_ref[pl.ds(HALO - 1, tl), :] * w[0:1] + xe_ref[pl.ds(HALO, tl), :] * w[1:2]
         + xe_ref[pl.ds(HALO + 1, tl), :] * w[2:3] + xe_ref[pl.ds(HALO + 2, tl), :] * w[3:4])
    y = _silu(y)
    qscale = jnp.where(s == 0, GDN_DK ** -0.5, 1.0)
    for h in range(GDN_HEADS):
        cs = slice(h * GDN_DK, (h + 1) * GDN_DK)
        yh = y[:, cs]
        nrm = lax.rsqrt(jnp.sum(yh * yh, axis=-1, keepdims=True) + EPS) * qscale
        fac = jnp.where(s == 2, 1.0, nrm)
        o_ref[0, 0, :, cs] = (yh * fac).astype(o_ref.dtype)


def _gdn_prep(p3, conv_w, batch, seq):
    nblk = seq // GPREP_TL
    hb = GPREP_TL // HALO
    last = seq // HALO - 1
    base = P_NQ // 512
    return pl.pallas_call(
        _gdn_prep_kernel,
        grid=(batch, nblk, 3),
        in_specs=[pl.BlockSpec((1, GPREP_TL, 512), lambda b, j, s: (b, j, base + s)),
                  pl.BlockSpec((1, HALO, 512), lambda b, j, s: (b, jnp.maximum(j * hb - 1, 0), base + s)),
                  pl.BlockSpec((1, HALO, 512), lambda b, j, s: (b, jnp.minimum((j + 1) * hb, last), base + s)),
                  pl.BlockSpec((CONV_W, 512), lambda b, j, s: (0, s))],
        out_specs=pl.BlockSpec((1, 1, GPREP_TL, 512), lambda b, j, s: (s, b, j, 0)),
        out_shape=jax.ShapeDtypeStruct((3, batch, seq, 512), BF16),
        scratch_shapes=[pltpu.VMEM((GPREP_TL + 2 * HALO, 512), F32)],
        compiler_params=_cparams(("parallel", "parallel", "arbitrary")),
        name="gdn_prep",
    )(p3, p3, p3, conv_w)


GDN_TL = 512
GDN_CB = 4


def _gdn_kernel(q_ref, k_ref, v_ref, sm_ref, sel_ref, nega_ref, dtb_ref, o_ref, s_ref):
    d = pl.program_id(0)
    nc = GDN_TL // CHUNK

    @pl.when(pl.program_id(2) == 0)
    def _():
        s_ref[...] = jnp.zeros_like(s_ref)

    incl, strict = _tri_masks(d)
    tri = jnp.where(incl, 1.0, 0.0).astype(BF16)
    tri3 = jnp.concatenate([tri, tri, tri], axis=1)
    upper = jnp.where(strict, 1.0, 0.0)
    r_i = lax.broadcasted_iota(jnp.int32, (CHUNK, CHUNK), 0)
    c_i = lax.broadcasted_iota(jnp.int32, (CHUNK, CHUNK), 1)
    eye = jnp.where(r_i == c_i, 1.0, 0.0)
    fwd = d == 0
    H = GDN_HEADS

    def group(gi, carry):
        rows = []
        for ci in range(GDN_CB):
            c = gi * GDN_CB + ci
            ce = jnp.where(fwd, c, nc - 1 - c)
            rows.append(pl.ds(pl.multiple_of(ce * CHUNK, CHUNK), CHUNK))
        sel = [_dot(sm_ref[0, r, :], sel_ref[0]) for r in rows]
        beta = [_sigmoid(x) for x in sel]
        g = [nega_ref[0] * _softplus(x + dtb_ref[0]) for x in sel]
        g2 = [jnp.concatenate([x[:, H + h:H + h + 1] * upper for h in range(H)] + [x], axis=1) for x in g]
        dd = [_dot(tri3, _split3_rows(x)) for x in g2]
        ch = [(ci, h) for ci in range(GDN_CB) for h in range(H)]
        hs = [slice(h * GDN_DK, (h + 1) * GDN_DK) for h in range(H)]
        k_a = [k_ref[0, 0, rows[ci], hs[h]] for ci, h in ch]
        q_a = [q_ref[0, 0, rows[ci], hs[h]] for ci, h in ch]
        v_a = [v_ref[0, 0, rows[ci], hs[h]] for ci, h in ch]
        n_ch = range(len(ch))
        kf = [k_a[i].astype(F32) for i in n_ch]
        gam = [jnp.where(incl, jnp.exp(dd[ci][:, h * CHUNK:(h + 1) * CHUNK]), 0.0) for ci, h in ch]
        bcol = [dd[ci][:, H * CHUNK + H + h:H * CHUNK + H + h + 1] for ci, h in ch]
        blast = [jnp.where(fwd, b[CHUNK - 1:CHUNK], b[0:1]) for b in bcol]
        eb = [jnp.exp(b) for b in bcol]
        bet = [beta[ci][:, h:h + 1] for ci, h in ch]
        kk = [_dot_nt(k_a[i], k_a[i]) for i in n_ch]
        qk = [_dot_nt(q_a[i], k_a[i]) for i in n_ch]
        pw = [jnp.where(strict, -(bet[i] * kk[i] * gam[i]), 0.0) for i in n_ch]
        t = [eye + pw[i] for i in n_ch]
        for _ in range(5):
            pw = [_dot_split(pw[i], pw[i]) for i in n_ch]
            t = [t[i] + _dot_split(t[i], pw[i]) for i in n_ch]
        rhs = [jnp.concatenate([v_a[i].astype(F32) * bet[i], kf[i] * (bet[i] * eb[i])], axis=1).astype(BF16)
               for i in n_ch]
        sol = [_dot(t[i].astype(BF16), rhs[i]) for i in n_ch]
        u = [sol[i][:, :GDN_DV] for i in n_ch]
        w = [sol[i][:, GDN_DV:].astype(BF16) for i in n_ch]
        attn = [(qk[i] * gam[i]).astype(BF16) for i in n_ch]
        qdec = [(q_a[i].astype(F32) * eb[i]).astype(BF16) for i in n_ch]
        kend = [(kf[i] * jnp.exp(blast[i] - bcol[i])).astype(BF16) for i in n_ch]
        dec = [jnp.exp(b) for b in blast]
        st = [s_ref[h] for h in range(H)]
        for ci in range(GDN_CB):
            ix = [ci * H + h for h in range(H)]
            st_b = [st[h].astype(BF16) for h in range(H)]
            vnb = [(u[i] - _dot(w[i], st_b[h])).astype(BF16) for h, i in enumerate(ix)]
            o_new = [(_dot(qdec[i], st_b[h]) + _dot(attn[i], vnb[h])).astype(o_ref.dtype) for h, i in enumerate(ix)]
            st = [dec[i] * st[h] + _dot_tn(kend[i], vnb[h]) for h, i in enumerate(ix)]
            for h in range(H):
                o_ref[0, 0, rows[ci], hs[h]] = o_new[h]
        for h in range(H):
            s_ref[h] = st[h]
        return carry

    lax.fori_loop(0, nc // GDN_CB, group, 0)


def _gdn(qkv, p3, sel_m, nega, dtb, batch, seq):
    nblk = seq // GDN_TL

    def blk(dd, j):
        return j + dd * (nblk - 1 - 2 * j)

    def qspec(idx):
        return pl.BlockSpec((1, 1, GDN_TL, 512), lambda dd, b, j: (idx, b, blk(dd, j), 0))

    return pl.pallas_call(
        _gdn_kernel,
        grid=(2, batch, nblk),
        in_specs=[qspec(0), qspec(1), qspec(2),
                  pl.BlockSpec((1, GDN_TL, LANE), lambda dd, b, j: (b, blk(dd, j), P_SMALL // LANE)),
                  pl.BlockSpec((1, LANE, LANE), lambda dd, b, j: (dd, 0, 0)),
                  pl.BlockSpec((1, 1, LANE), lambda dd, b, j: (dd, 0, 0)),
                  pl.BlockSpec((1, 1, LANE), lambda dd, b, j: (dd, 0, 0))],
        out_specs=pl.BlockSpec((1, 1, GDN_TL, 512), lambda dd, b, j: (dd, b, blk(dd, j), 0)),
        out_shape=jax.ShapeDtypeStruct((2, batch, seq, 512), BF16),
        scratch_shapes=[pltpu.VMEM((GDN_HEADS, GDN_DK, GDN_DV), F32)],
        compiler_params=_cparams(("parallel", "parallel", "arbitrary")),
        name="gdn",
    )(qkv, qkv, qkv, p3, sel_m, nega, dtb)


MERGE_TM = 512


def _head_rms(o, w):
    parts = []
    for h in range(4):
        oh = o[:, h * LANE:(h + 1) * LANE]
        parts.append(oh * lax.rsqrt(jnp.mean(oh * oh, axis=-1, keepdims=True) + EPS))
    return jnp.concatenate(parts, axis=1) * w


def _merge_kernel(gf_ref, gb_ref, gr_ref, od_ref, nf_ref, nb_ref, nz_ref, mg0_ref, mg1_ref, mg2_ref,
                  x_ref, g1_ref, nrm_ref, wb_ref, wo_ref, nw_ref, o_ref, *, diff_scale):
    nrm = nrm_ref[...]
    y_gla = _head_rms(gf_ref[0, 0].astype(F32) + gb_ref[0, 0].astype(F32), nrm[0:1]) * _silu(gr_ref[0].astype(F32))
    y_diff = _head_rms(od_ref[0].astype(F32), nrm[1:2]) * diff_scale
    y_gdn = _head_rms(nf_ref[0, 0].astype(F32) + nb_ref[0, 0].astype(F32), nrm[2:3]) * _silu(nz_ref[0].astype(F32))
    merged = (_sigmoid(mg0_ref[0].astype(F32)) * _dot(y_gla.astype(BF16), wb_ref[0])
              + _sigmoid(mg1_ref[0].astype(F32)) * _dot(y_diff.astype(BF16), wb_ref[1])
              + _sigmoid(mg2_ref[0].astype(F32)) * _dot(y_gdn.astype(BF16), wb_ref[2]))
    y = _dot(merged.astype(BF16), wo_ref[...])
    o_ref[0] = x_ref[0] + g1_ref[0] * _rms_rows(y, nw_ref[...])


def _merge(gla_o, p3, att_o, gdn_o, x3, g1, nrm3, wb, wo, nw, diff_scale, batch, seq):
    tm = MERGE_TM
    nblk = seq // tm
    d = D_MODEL

    def dirspec(idx):
        return pl.BlockSpec((1, 1, tm, 512), lambda b, j: (idx, b, j, 0))

    def pspec(width, col):
        return pl.BlockSpec((1, tm, width), lambda b, j: (b, j, col // width))

    return pl.pallas_call(
        functools.partial(_merge_kernel, diff_scale=diff_scale),
        grid=(batch, nblk),
        in_specs=[dirspec(0), dirspec(1), pspec(512, P_GR),
                  pl.BlockSpec((1, tm, 512), lambda b, j: (b, j, 0)),
                  dirspec(0), dirspec(1), pspec(512, P_NZ),
                  pspec(d, P_MG), pspec(d, P_MG + d), pspec(d, P_MG + 2 * d),
                  pl.BlockSpec((1, tm, d), lambda b, j: (b, j, 0)),
                  pl.BlockSpec((1, 1, d), lambda b, j: (b, 0, 0)),
                  pl.BlockSpec((3, 512), lambda b, j: (0, 0)),
                  pl.BlockSpec((3, 512, d), lambda b, j: (0, 0, 0)),
                  pl.BlockSpec((d, d), lambda b, j: (0, 0)),
                  pl.BlockSpec((1, d), lambda b, j: (0, 0))],
        out_specs=pl.BlockSpec((1, tm, d), lambda b, j: (b, j, 0)),
        out_shape=jax.ShapeDtypeStruct((batch, seq, d), F32),
        compiler_params=_cparams(("parallel", "parallel")),
        name="merge",
    )(gla_o, gla_o, p3, att_o, gdn_o, gdn_o, p3, p3, p3, p3, x3, g1, nrm3, wb, wo, nw)


FFN_TM = 512


def _ffn_kernel(x_ref, nw_ref, sc_ref, sh_ref, w1_ref, w3_ref, w2_ref, g2_ref, nwo_ref, o_ref, h_ref, acc_ref):
    j = pl.program_id(1)

    @pl.when(j == 0)
    def _():
        y = _rms_rows(x_ref[...], nw_ref[...])
        h_ref[...] = (y * (1.0 + sc_ref[0]) + sh_ref[0]).astype(BF16)
        acc_ref[...] = jnp.zeros_like(acc_ref)

    h = h_ref[...]
    a = _dot(h, w1_ref[...])
    b = _dot(h, w3_ref[...])
    acc_ref[...] += _dot((_silu(a) * b).astype(BF16), w2_ref[...])

    @pl.when(j == pl.num_programs(1) - 1)
    def _():
        o_ref[...] = x_ref[...] + g2_ref[0] * _rms_rows(acc_ref[...], nwo_ref[...])


def _ffn(x2, nw, sc, sh, w1, w3, w2, g2, nwo, seq):
    m, d = x2.shape
    f = w1.shape[1]
    tm, tf = FFN_TM, f // 2
    per_b = seq // tm
    vec = pl.BlockSpec((1, d), lambda i, j: (0, 0))
    mod = pl.BlockSpec((1, 1, d), lambda i, j: (i // per_b, 0, 0))
    return pl.pallas_call(
        _ffn_kernel,
        grid=(m // tm, f // tf),
        in_specs=[pl.BlockSpec((tm, d), lambda i, j: (i, 0)), vec, mod, mod,
                  pl.BlockSpec((d, tf), lambda i, j: (0, j)),
                  pl.BlockSpec((d, tf), lambda i, j: (0, j)),
                  pl.BlockSpec((tf, d), lambda i, j: (j, 0)),
                  mod, vec],
        out_specs=pl.BlockSpec((tm, d), lambda i, j: (i, 0)),
        out_shape=jax.ShapeDtypeStruct((m, d), F32),
        scratch_shapes=[pltpu.VMEM((tm, d), BF16), pltpu.VMEM((tm, d), F32)],
        compiler_params=_cparams(("parallel", "arbitrary")),
        name="ffn",
    )(x2, nw, sc, sh, w1, w3, w2, g2, nwo)


ROUTER_TM = 1024


def _router_kernel(x_ref, nw_ref, sc_ref, sh_ref, rw_ref, o_ref, h_ref):
    y = _rms_rows(x_ref[...], nw_ref[...])
    h = y * (1.0 + sc_ref[0]) + sh_ref[0]
    h_ref[...] = h
    logits = jnp.dot(h, rw_ref[...], preferred_element_type=F32, precision=lax.Precision.HIGHEST)
    lane = lax.broadcasted_iota(jnp.int32, logits.shape, 1)
    neg = -jnp.inf
    logits = jnp.where(lane < N_EXPERTS, logits, neg)
    m1 = jnp.max(logits, axis=-1, keepdims=True)
    i1 = jnp.min(jnp.where(logits == m1, lane, LANE), axis=-1, keepdims=True)
    rest = jnp.where(lane == i1, neg, logits)
    m2 = jnp.max(rest, axis=-1, keepdims=True)
    i2 = jnp.min(jnp.where(rest == m2, lane, LANE), axis=-1, keepdims=True)
    e = jnp.exp(m2 - m1)
    w1 = 1.0 / (1.0 + e)
    w2 = e / (1.0 + e)
    o_ref[...] = (jnp.where(lane == 0, i1.astype(F32), 0.0) + jnp.where(lane == 1, i2.astype(F32), 0.0)
                  + jnp.where(lane == 2, w1, 0.0) + jnp.where(lane == 3, w2, 0.0))


def _router(x2, nw, sc, sh, rw_pad, seq):
    m, d = x2.shape
    tm = ROUTER_TM
    per_b = seq // tm
    return pl.pallas_call(
        _router_kernel,
        grid=(m // tm,),
        in_specs=[pl.BlockSpec((tm, d), lambda i: (i, 0)),
                  pl.BlockSpec((1, d), lambda i: (0, 0)),
                  pl.BlockSpec((1, 1, d), lambda i: (i // per_b, 0, 0)),
                  pl.BlockSpec((1, 1, d), lambda i: (i // per_b, 0, 0)),
                  pl.BlockSpec((d, LANE), lambda i: (0, 0))],
        out_specs=[pl.BlockSpec((tm, LANE), lambda i: (i, 0)), pl.BlockSpec((tm, d), lambda i: (i, 0))],
        out_shape=[jax.ShapeDtypeStruct((m, LANE), F32), jax.ShapeDtypeStruct((m, d), F32)],
        compiler_params=_cparams(("parallel",)),
        name="router",
    )(x2, nw, sc, sh, rw_pad)


MOE_TM = 512
MOE_TF = 512


def _moe_plan(rt, m):
    tm = MOE_TM
    n_assign = TOP_K * m
    nt = n_assign // tm + N_EXPERTS
    e_all = rt[:, :TOP_K].astype(jnp.int32).reshape(-1)
    onehot = (e_all[:, None] == jnp.arange(N_EXPERTS, dtype=jnp.int32)[None, :]).astype(jnp.int32)
    csum = jnp.cumsum(onehot, axis=0)
    rank = jnp.take_along_axis(csum, e_all[:, None], axis=1)[:, 0] - 1
    counts = csum[-1]
    padded = ((counts + tm - 1) // tm) * tm
    ends = jnp.cumsum(padded)
    dest = (ends - padded)[e_all] + rank
    src = jnp.full((nt * tm,), -1, jnp.int32).at[dest].set(jnp.arange(n_assign, dtype=jnp.int32))
    tile_expert = jnp.minimum(jnp.searchsorted(ends, jnp.arange(nt, dtype=jnp.int32) * tm, side="right"),
                              N_EXPERTS - 1).astype(jnp.int32)
    n_used = (ends[-1] // tm).astype(jnp.int32).reshape(1)
    return src.reshape(nt, 1, tm), tile_expert, n_used


def _moe_kernel(te_ref, nu_ref, src_ref, nxt_ref, h_hbm, w1_ref, w3_ref, w2_ref, y_hbm,
                gbuf, obuf, hb_ref, acc_ref, gsem, ssem):
    t = pl.program_id(0)
    j = pl.program_id(1)
    nt = pl.num_programs(0)
    nf = pl.num_programs(1)
    tm = MOE_TM
    slot = t % 2
    n_assign = y_hbm.shape[0] - 2 * tm

    def gather_copy(idx_ref, r, sl):
        tok = jnp.maximum(idx_ref[0, 0, r], 0) // TOP_K
        return pltpu.make_async_copy(h_hbm.at[pl.ds(tok, 1)], gbuf.at[sl, pl.ds(r, 1)], gsem.at[sl])

    def scatter_copy(r, par):
        a = src_ref[0, 0, r]
        row = jnp.where(a >= 0, a, n_assign + par * tm + r)
        return pltpu.make_async_copy(obuf.at[pl.ds(r, 1)], y_hbm.at[pl.ds(row, 1)], ssem.at[0])

    def for_rows(fn):
        def body(r, c):
            fn(r)
            return c
        lax.fori_loop(0, tm, body, 0)

    @pl.when(j == 0)
    def _():
        @pl.when(t == 0)
        def _():
            for_rows(lambda r: gather_copy(src_ref, r, 0).start())

        for_rows(lambda r: gather_copy(src_ref, r, slot).wait())

        @pl.when(t + 1 < nt)
        def _():
            for_rows(lambda r: gather_copy(nxt_ref, r, 1 - slot).start())

        hb_ref[...] = gbuf[slot].astype(BF16)
        acc_ref[...] = jnp.zeros_like(acc_ref)

    @pl.when(t < nu_ref[0])
    def _():
        h = hb_ref[...]
        a = _dot(h, w1_ref[0])
        b = _dot(h, w3_ref[0])
        acc_ref[...] += _dot((_silu(a) * b).astype(BF16), w2_ref[0])

    @pl.when(j == nf - 1)
    def _():
        @pl.when(t > 0)
        def _():
            for_rows(lambda r: scatter_copy(r, 1 - slot).wait())

        obuf[...] = acc_ref[...]
        for_rows(lambda r: scatter_copy(r, slot).start())

        @pl.when(t == nt - 1)
        def _():
            for_rows(lambda r: scatter_copy(r, slot).wait())


def _moe(h, src, tile_expert, n_used, w1, w3, w2):
    m, d = h.shape
    ne, _, f = w1.shape
    tm, tf = MOE_TM, MOE_TF
    nt = src.shape[0]

    def smem_tile(idx):
        return pl.BlockSpec((1, 1, tm), idx, memory_space=pltpu.SMEM)

    grid_spec = pltpu.PrefetchScalarGridSpec(
        num_scalar_prefetch=2,
        grid=(nt, f // tf),
        in_specs=[smem_tile(lambda t, j, te, nu: (t, 0, 0)),
                  smem_tile(lambda t, j, te, nu: (jnp.minimum(t + 1, nt - 1), 0, 0)),
                  pl.BlockSpec(memory_space=pl.ANY),
                  pl.BlockSpec((1, d, tf), lambda t, j, te, nu: (te[t], 0, j)),
                  pl.BlockSpec((1, d, tf), lambda t, j, te, nu: (te[t], 0, j)),
                  pl.BlockSpec((1, tf, d), lambda t, j, te, nu: (te[t], j, 0))],
        out_specs=pl.BlockSpec(memory_space=pl.ANY),
        scratch_shapes=[pltpu.VMEM((2, tm, d), F32), pltpu.VMEM((tm, d), F32), pltpu.VMEM((tm, d), BF16),
                        pltpu.VMEM((tm, d), F32), pltpu.SemaphoreType.DMA((2,)), pltpu.SemaphoreType.DMA((1,))])
    return pl.pallas_call(
        _moe_kernel,
        grid_spec=grid_spec,
        out_shape=jax.ShapeDtypeStruct((TOP_K * m + 2 * tm, d), F32),
        compiler_params=_cparams(("arbitrary", "arbitrary")),
        name="moe",
    )(tile_expert, n_used, src, src, h, w1, w3, w2)


COMBINE_TM = 512


def _combine_kernel(y_ref, rt_ref, x_ref, g2_ref, nwo_ref, o_ref):
    d = x_ref.shape[1]
    rt = rt_ref[...]
    y = rt[:, 2:3] * y_ref[:, :d] + rt[:, 3:4] * y_ref[:, d:]
    o_ref[...] = x_ref[...] + g2_ref[0] * _rms_rows(y, nwo_ref[...])


def _combine(y2, rt, x2, g2, nwo, seq):
    m, d = x2.shape
    tm = COMBINE_TM
    per_b = seq // tm
    return pl.pallas_call(
        _combine_kernel,
        grid=(m // tm,),
        in_specs=[pl.BlockSpec((tm, TOP_K * d), lambda i: (i, 0)),
                  pl.BlockSpec((tm, LANE), lambda i: (i, 0)),
                  pl.BlockSpec((tm, d), lambda i: (i, 0)),
                  pl.BlockSpec((1, 1, d), lambda i: (i // per_b, 0, 0)),
                  pl.BlockSpec((1, d), lambda i: (0, 0))],
        out_specs=pl.BlockSpec((tm, d), lambda i: (i, 0)),
        out_shape=jax.ShapeDtypeStruct((m, d), F32),
        compiler_params=_cparams(("parallel",)),
        name="moe_combine",
    )(y2, rt, x2, g2, nwo)


def _reorder_w_in(w):
    o = _IN_OFF

    def cols(a, b):
        return w[:, o[a]:o[b]]

    small = jnp.concatenate([cols(4, 6), cols(13, 17)], axis=1)
    pad = jnp.zeros((w.shape[0], P_COLS - P_SMALL - small.shape[1]), w.dtype)
    out = jnp.concatenate([cols(6, 9),
                           cols(2, 4),
                           cols(0, 2),
                           cols(9, 13),
                           cols(17, 18),
                           small, pad], axis=1)
    return out.astype(BF16)


def _gla_gate_params(up, bias):
    hi = up.astype(BF16)
    lo = (up - hi.astype(F32)).astype(BF16)
    out = jnp.zeros((2, 2 * LANE, GLA_HEADS * GLA_DK), BF16)
    for dd, off in ((0, S_GWF), (1, S_GWB)):
        out = out.at[dd, off:off + GLA_RANK].set(hi[dd])
        out = out.at[dd, LANE + off:LANE + off + GLA_RANK].set(lo[dd])
    return out, bias.reshape(2, 1, -1).astype(F32)


def _gdn_gate_params(a_log, dt_bias):
    h = GDN_HEADS
    sel = np.zeros((2, LANE, LANE), np.float32)
    for dd, (boff, aoff) in enumerate(((S_NBF, S_NAF), (S_NBB, S_NAB))):
        for i in range(h):
            sel[dd, boff + i, i] = 1.0
            sel[dd, aoff + i, h + i] = 1.0
    nega = jnp.zeros((2, 1, LANE), F32).at[:, 0, h:2 * h].set(-jnp.exp(a_log.astype(F32)))
    dtb = jnp.zeros((2, 1, LANE), F32).at[:, 0, h:2 * h].set(dt_bias.astype(F32))
    return jnp.asarray(sel, BF16), nega, dtb


def _rope_inv_row():
    inv = ROPE_THETA ** (-jnp.arange(0, ROT_DIM, 2, dtype=F32) / ROT_DIM)
    lane = np.arange(LANE) % DIFF_DH
    row = jnp.where(jnp.asarray(lane < ROT_DIM), inv[jnp.asarray(lane % (ROT_DIM // 2))], 0.0)
    return row.reshape(1, LANE).astype(F32)


def kernel(x, c, positions, adaln_w, adaln_b, norm_w, w_in, gla_gate_up, gla_gate_bias, gla_norm, diff_lambda,
           diff_norm, gdn_conv, gdn_A_log, gdn_dt_bias, gdn_norm, w_branch, w_out, ffn_w1, ffn_w3, ffn_w2,
           router_w, moe_w1, moe_w3, moe_w2):
    batch, seq, d = x.shape
    depth = adaln_w.shape[0]
    m = batch * seq

    c_pad = jnp.zeros((8, d), F32).at[:batch].set(c)
    mod = _adaln(c_pad, adaln_w, adaln_b)[:, :batch]
    pos3 = positions.reshape(batch, seq, 1)
    inv_row = _rope_inv_row()

    for layer in range(depth):
        sh1, sc1, g1, sh2, sc2, g2 = [t.reshape(batch, 1, d) for t in jnp.split(mod[layer], 6, axis=-1)]
        nw = norm_w[layer].astype(F32)

        p = _inproj(x.reshape(m, d), nw[0:1], sc1, sh1, _reorder_w_in(w_in[layer]), seq)
        p3 = p.reshape(batch, seq, P_COLS)

        up2, gbias = _gla_gate_params(gla_gate_up[layer], gla_gate_bias[layer])
        gla_o = _gla(p3, up2, gbias, batch, seq)

        qk = _rope(p3, pos3, inv_row, batch, seq)
        lam_init = 0.8 - 0.6 * math.exp(-0.3 * layer)
        att_o = _attn(qk, p3, diff_lambda[layer].astype(F32), lam_init, batch, seq)

        qkv = _gdn_prep(p3, gdn_conv[layer].astype(F32), batch, seq)
        sel_m, nega, dtb = _gdn_gate_params(gdn_A_log[layer], gdn_dt_bias[layer])
        gdn_o = _gdn(qkv, p3, sel_m, nega, dtb, batch, seq)

        nrm3 = jnp.stack([jnp.tile(gla_norm[layer], 4), jnp.tile(diff_norm[layer], 4),
                          jnp.tile(gdn_norm[layer], 4)]).astype(F32)
        x = _merge(gla_o, p3, att_o, gdn_o, x, g1, nrm3, w_branch[layer].astype(BF16),
                   w_out[layer].astype(BF16), nw[1:2], 1.0 - lam_init, batch, seq)

        x2 = x.reshape(m, d)
        if layer % 2 == 0:
            li = layer // 2
            x2 = _ffn(x2, nw[2:3], sc2, sh2, ffn_w1[li].astype(BF16), ffn_w3[li].astype(BF16),
                      ffn_w2[li].astype(BF16), g2, nw[3:4], seq)
        else:
            li = layer // 2
            rw_pad = jnp.zeros((d, LANE), F32).at[:, :N_EXPERTS].set(router_w[li])
            rt, h2 = _router(x2, nw[2:3], sc2, sh2, rw_pad, seq)
            src, tile_expert, n_used = _moe_plan(rt, m)
            y = _moe(h2, src, tile_expert, n_used, moe_w1[li].astype(BF16), moe_w3[li].astype(BF16),
                     moe_w2[li].astype(BF16))
            x2 = _combine(y.reshape(m + MOE_TM, TOP_K * d), rt, x2, g2, nw[3:4], seq)
        x = x2.reshape(batch, seq, d)
    return x
```

```python
import functools
import math

import jax
import jax.numpy as jnp
import numpy as np
from jax import lax
from jax.experimental import pallas as pl
from jax.experimental.pallas import tpu as pltpu

F32 = jnp.float32
BF16 = jnp.bfloat16

D_MODEL = 1024
GLA_HEADS, GLA_DK, GLA_DV, GLA_RANK, GLA_TAU = 4, 64, 128, 16, 16.0
DIFF_HEADS, DIFF_DH = 4, 64
ROT_DIM = DIFF_DH // 4
ROPE_THETA = 500000.0
GDN_HEADS, GDN_DK, GDN_DV, CONV_W = 4, 128, 128, 4
CHUNK = 64
N_BRANCH = 3
N_EXPERTS, TOP_K = 8, 2
EPS = 1e-6
BRANCH_W = 512

_IN_SIZES = (256, 256, 512, 512, 16, 16, 512, 512, 512, 512, 512, 512, 512, 4, 4, 4, 4, 3 * D_MODEL)
_IN_OFF = tuple(int(v) for v in np.concatenate([[0], np.cumsum(_IN_SIZES)]))

P_DQ, P_DK, P_DV = 0, 512, 1024
P_GV, P_GR = 1536, 2048
P_GQ, P_GK = 2560, 2816
P_NQ, P_NK, P_NV, P_NZ = 3072, 3584, 4096, 4608
P_MG = 5120
P_SMALL = 8192
P_COLS = 8448
S_GWF, S_GWB, S_NBF, S_NBB, S_NAF, S_NAB = 0, 16, 32, 36, 40, 44

VMEM_LIMIT = 56 * 1024 * 1024
LANE = 128


def _cparams(sem):
    return pltpu.CompilerParams(dimension_semantics=sem, vmem_limit_bytes=VMEM_LIMIT)


def _dot(a, b):
    return jnp.dot(a, b, preferred_element_type=F32)


def _dot_nt(a, b):
    return lax.dot_general(a, b, (((1,), (1,)), ((), ())), preferred_element_type=F32)


def _dot_tn(a, b):
    return lax.dot_general(a, b, (((0,), (0,)), ((), ())), preferred_element_type=F32)


def _split3_rows(x):
    hi = x.astype(BF16)
    r1 = x - hi.astype(F32)
    mid = r1.astype(BF16)
    lo = (r1 - mid.astype(F32)).astype(BF16)
    return jnp.concatenate([hi, mid, lo], axis=0)


def _dot_split(a, b):
    a_hi = a.astype(BF16)
    a_lo = (a - a_hi.astype(F32)).astype(BF16)
    b_hi = b.astype(BF16)
    b_lo = (b - b_hi.astype(F32)).astype(BF16)
    return _dot(jnp.concatenate([a_hi, a_lo, a_hi], axis=1), jnp.concatenate([b_hi, b_hi, b_lo], axis=0))


def _sigmoid(x):
    return 1.0 / (1.0 + jnp.exp(-x))


def _silu(x):
    return x * _sigmoid(x)


def _softplus(x):
    return jnp.maximum(x, 0.0) + jnp.log1p(jnp.exp(-jnp.abs(x)))


def _rms_rows(x, w):
    return x * lax.rsqrt(jnp.mean(x * x, axis=-1, keepdims=True) + EPS) * w


def _tri_masks(d):
    r = lax.broadcasted_iota(jnp.int32, (CHUNK, CHUNK), 0)
    c = lax.broadcasted_iota(jnp.int32, (CHUNK, CHUNK), 1)
    lag = (r - c) * (1 - 2 * d)
    return lag >= 0, lag > 0


def _adaln_kernel(c_ref, w_ref, b_ref, o_ref):
    c = c_ref[...]
    o_ref[0] = jnp.dot(_silu(c), w_ref[0], preferred_element_type=F32,
                       precision=lax.Precision.HIGHEST) + b_ref[0]


def _adaln(c_pad, adaln_w, adaln_b):
    depth, d, n = adaln_w.shape
    tn = 1536
    return pl.pallas_call(
        _adaln_kernel,
        grid=(depth, n // tn),
        in_specs=[pl.BlockSpec((8, d), lambda l, j: (0, 0)),
                  pl.BlockSpec((1, d, tn), lambda l, j: (l, 0, j)),
                  pl.BlockSpec((1, 1, tn), lambda l, j: (l, 0, j))],
        out_specs=pl.BlockSpec((1, 8, tn), lambda l, j: (l, 0, j)),
        out_shape=jax.ShapeDtypeStruct((depth, 8, n), F32),
        compiler_params=_cparams(("parallel", "parallel")),
        name="adaln",
    )(c_pad, adaln_w, adaln_b.reshape(depth, 1, n))


def _inproj_kernel(x_ref, nw_ref, sc_ref, sh_ref, w_ref, o_ref, h_ref):
    @pl.when(pl.program_id(1) == 0)
    def _():
        y = _rms_rows(x_ref[...], nw_ref[...])
        h_ref[...] = (y * (1.0 + sc_ref[0]) + sh_ref[0]).astype(BF16)

    o_ref[...] = _dot(h_ref[...], w_ref[...]).astype(o_ref.dtype)


def _inproj(x2, nw, sc, sh, w, seq):
    m, d = x2.shape
    n = w.shape[1]
    tm, tn = 1024, 768
    per_b = seq // tm
    return pl.pallas_call(
        _inproj_kernel,
        grid=(m // tm, n // tn),
        in_specs=[pl.BlockSpec((tm, d), lambda i, j: (i, 0)),
                  pl.BlockSpec((1, d), lambda i, j: (0, 0)),
                  pl.BlockSpec((1, 1, d), lambda i, j: (i // per_b, 0, 0)),
                  pl.BlockSpec((1, 1, d), lambda i, j: (i // per_b, 0, 0)),
                  pl.BlockSpec((d, tn), lambda i, j: (0, j))],
        out_specs=pl.BlockSpec((tm, tn), lambda i, j: (i, j)),
        out_shape=jax.ShapeDtypeStruct((m, n), BF16),
        scratch_shapes=[pltpu.VMEM((tm, d), BF16)],
        compiler_params=_cparams(("parallel", "arbitrary")),
        name="inproj",
    )(x2, nw, sc, sh, w)


GLA_TL = 512
GLA_CB = 8


def _gla_kernel(q_ref, k_ref, v_ref, sm_ref, up_ref, bias_ref, o_ref, st_ref):
    d = pl.program_id(0)
    nc = GLA_TL // CHUNK

    @pl.when(pl.program_id(2) == 0)
    def _():
        st_ref[...] = jnp.zeros_like(st_ref)

    incl, _ = _tri_masks(d)
    tri = jnp.where(incl, 1.0, 0.0).astype(BF16)
    tri3 = jnp.concatenate([tri, tri, tri], axis=1)
    fwd = d == 0
    scale = GLA_DK ** -0.5

    H = GLA_HEADS
    ks = [slice(h * GLA_DK, (h + 1) * GLA_DK) for h in range(H)]
    vs = [slice(h * GLA_DV, (h + 1) * GLA_DV) for h in range(H)]

    def group(gi, carry):
        rows = []
        for ci in range(GLA_CB):
            c = gi * GLA_CB + ci
            ce = jnp.where(fwd, c, nc - 1 - c)
            rows.append(pl.ds(pl.multiple_of(ce * CHUNK, CHUNK), CHUNK))
        cr = range(GLA_CB)
        sm = [sm_ref[0, r, :] for r in rows]
        z = [_dot(jnp.concatenate([x, x], axis=1), up_ref[0]) + bias_ref[0] for x in sm]
        g = [(jnp.minimum(x, 0.0) - jnp.log1p(jnp.exp(-jnp.abs(x)))) * (1.0 / GLA_TAU) for x in z]
        b = [_dot(tri3, _split3_rows(x)) for x in g]
        b_last = [jnp.where(fwd, x[CHUNK - 1:CHUNK], x[0:1]) for x in b]
        b_mid = [jnp.where(fwd, x[CHUNK // 2 - 1:CHUNK // 2], x[CHUNK // 2:CHUNK // 2 + 1]) for x in b]
        q = [q_ref[0, r, :].astype(F32) * scale for r in rows]
        k = [k_ref[0, r, :].astype(F32) for r in rows]
        qe = [(q[i] * jnp.exp(b[i] - b_mid[i])).astype(BF16) for i in cr]
        ke = [(k[i] * jnp.exp(b_mid[i] - b[i])).astype(BF16) for i in cr]
        kend = [(k[i] * jnp.exp(b_last[i] - b[i])).astype(BF16) for i in cr]
        qd = [(q[i] * jnp.exp(b[i])).astype(BF16) for i in cr]
        dec = [jnp.exp(x) for x in b_last]
        ch = [(ci, h) for ci in cr for h in range(H)]
        v_a = [v_ref[0, rows[ci], vs[h]] for ci, h in ch]
        a = [jnp.where(incl, _dot_nt(qe[ci][:, ks[h]], ke[ci][:, ks[h]]), 0.0).astype(BF16) for ci, h in ch]
        o_in = [_dot(a[i], v_a[i]) for i in range(len(ch))]
        kv = [_dot_tn(v_a[i], kend[ci][:, ks[h]]) for i, (ci, h) in enumerate(ch)]
        st = st_ref[...]
        for ci in cr:
            st_b = st.astype(BF16)
            for h in range(H):
                o = o_in[ci * H + h] + _dot_nt(qd[ci][:, ks[h]], st_b[:, ks[h]])
                o_ref[0, 0, rows[ci], vs[h]] = o.astype(o_ref.dtype)
            st = st * dec[ci] + jnp.concatenate([kv[ci * H + h] for h in range(H)], axis=1)
        st_ref[...] = st
        return carry

    lax.fori_loop(0, nc // GLA_CB, group, 0)


def _gla(p3, up2, bias, batch, seq):
    nblk = seq // GLA_TL

    def blk(dd, j):
        return j + dd * (nblk - 1 - 2 * j)

    return pl.pallas_call(
        _gla_kernel,
        grid=(2, batch, nblk),
        in_specs=[pl.BlockSpec((1, GLA_TL, 256), lambda dd, b, j: (b, blk(dd, j), P_GQ // 256)),
                  pl.BlockSpec((1, GLA_TL, 256), lambda dd, b, j: (b, blk(dd, j), P_GK // 256)),
                  pl.BlockSpec((1, GLA_TL, 512), lambda dd, b, j: (b, blk(dd, j), P_GV // 512)),
                  pl.BlockSpec((1, GLA_TL, LANE), lambda dd, b, j: (b, blk(dd, j), P_SMALL // LANE)),
                  pl.BlockSpec((1, 2 * LANE, 256), lambda dd, b, j: (dd, 0, 0)),
                  pl.BlockSpec((1, 1, 256), lambda dd, b, j: (dd, 0, 0))],
        out_specs=pl.BlockSpec((1, 1, GLA_TL, 512), lambda dd, b, j: (dd, b, blk(dd, j), 0)),
        out_shape=jax.ShapeDtypeStruct((2, batch, seq, 512), BF16),
        scratch_shapes=[pltpu.VMEM((GLA_DV, GLA_HEADS * GLA_DK), F32)],
        compiler_params=_cparams(("parallel", "parallel", "arbitrary")),
        name="gla",
    )(p3, p3, p3, p3, up2, bias)


ROPE_TL = 512


def _rope_kernel(q_ref, k_ref, pos_ref, inv_ref, o_ref):
    ang = pos_ref[0].astype(F32) * inv_ref[...]
    cosf = jnp.cos(ang)
    sinf = jnp.sin(ang)
    lane = lax.broadcasted_iota(jnp.int32, (1, LANE), 1) % DIFF_DH
    half = ROT_DIM // 2
    cmul = jnp.where(lane < ROT_DIM, cosf, 1.0)
    s_up = jnp.where(lane < half, -sinf, 0.0)
    s_dn = jnp.where((lane >= half) & (lane < ROT_DIM), sinf, 0.0)
    for s, (x_ref, scale) in enumerate(((q_ref, DIFF_DH ** -0.5 * math.log2(math.e)), (k_ref, 1.0))):
        for gidx in range(4):
            cs = slice(gidx * LANE, (gidx + 1) * LANE)
            x = x_ref[0, :, cs].astype(F32)
            y = x * cmul + pltpu.roll(x, LANE - half, 1) * s_up + pltpu.roll(x, half, 1) * s_dn
            o_ref[s, 0, :, cs] = (y * scale).astype(o_ref.dtype)


def _rope(p3, pos3, inv_row, batch, seq):
    nblk = seq // ROPE_TL
    return pl.pallas_call(
        _rope_kernel,
        grid=(batch, nblk),
        in_specs=[pl.BlockSpec((1, ROPE_TL, 512), lambda b, j: (b, j, P_DQ // 512)),
                  pl.BlockSpec((1, ROPE_TL, 512), lambda b, j: (b, j, P_DK // 512)),
                  pl.BlockSpec((1, ROPE_TL, 1), lambda b, j: (b, j, 0)),
                  pl.BlockSpec((1, LANE), lambda b, j: (0, 0))],
        out_specs=pl.BlockSpec((2, 1, ROPE_TL, 512), lambda b, j: (0, b, j, 0)),
        out_shape=jax.ShapeDtypeStruct((2, batch, seq, 512), BF16),
        compiler_params=_cparams(("parallel", "parallel")),
        name="rope",
    )(p3, p3, pos3, inv_row)


ATT_TQ = 512
ATT_TK = 2048


def _attn_kernel(q_ref, k_ref, v_ref, lam_ref, o_ref, sa_ref, sb_ref, *, lam_init):
    q = q_ref[0, 0]
    lane = lax.broadcasted_iota(jnp.int32, q.shape, 1)
    zero = jnp.zeros_like(q)
    q1 = jnp.where(lane < DIFF_DH, q, zero)
    q2 = jnp.where(lane >= DIFF_DH, q, zero)
    tq = q.shape[0]
    tk = sa_ref.shape[2]
    nk = k_ref.shape[2] // tk
    ones_col = jnp.where(lax.broadcasted_iota(jnp.int32, (tk, LANE), 1) == 0, 1.0, 0.0).astype(BF16)

    def soft(s, v1, m, l, acc):
        m_new = jnp.maximum(m, jnp.max(s, axis=-1, keepdims=True))
        p = jnp.exp2((s - m_new).astype(BF16))
        alpha = jnp.exp2(m - m_new)
        pv = _dot(p, v1)
        return m_new, alpha * l + pv[:, LANE:LANE + 1], alpha * acc + pv[:, :LANE]

    def k_tile(t):
        return k_ref[0, 0, pl.ds(pl.multiple_of(t * tk, tk), tk), :]

    def v_tile(t):
        return jnp.concatenate([v_ref[0, pl.ds(pl.multiple_of(t * tk, tk), tk), :], ones_col], axis=1)

    def scores(t, dst):
        k = k_tile(t)
        dst[0] = _dot_nt(q1, k)
        dst[1] = _dot_nt(q2, k)

    def consume(t, src, carry):
        m1, l1, a1, m2, l2, a2 = carry
        v1 = v_tile(t)
        m1, l1, a1 = soft(src[0], v1, m1, l1, a1)
        m2, l2, a2 = soft(src[1], v1, m2, l2, a2)
        return m1, l1, a1, m2, l2, a2

    scores(0, sa_ref)

    def body(u, carry):
        t = 2 * u
        scores(t + 1, sb_ref)
        carry = consume(t, sa_ref, carry)
        scores(jnp.minimum(t + 2, nk - 1), sa_ref)
        return consume(t + 1, sb_ref, carry)

    neg = jnp.full((tq, 1), -jnp.inf, F32)
    zl = jnp.zeros((tq, 1), F32)
    za = jnp.zeros((tq, LANE), F32)
    m1, l1, a1, m2, l2, a2 = lax.fori_loop(0, nk // 2, body, (neg, zl, za, neg, zl, za))
    lp = lam_ref[...]
    lam = (jnp.exp(jnp.sum(lp[0:1] * lp[1:2], axis=-1, keepdims=True))
           - jnp.exp(jnp.sum(lp[2:3] * lp[3:4], axis=-1, keepdims=True)) + lam_init)
    o_ref[0] = (a1 / l1 - lam * (a2 / l2)).astype(o_ref.dtype)


def _attn(qk, p3, lam_p, lam_init, batch, seq):
    nq = seq // ATT_TQ
    tk = min(ATT_TK, seq // 2)
    return pl.pallas_call(
        functools.partial(_attn_kernel, lam_init=lam_init),
        grid=(batch, DIFF_HEADS, nq),
        in_specs=[pl.BlockSpec((1, 1, ATT_TQ, LANE), lambda b, h, i: (0, b, i, h)),
                  pl.BlockSpec((1, 1, seq, LANE), lambda b, h, i: (1, b, 0, h)),
                  pl.BlockSpec((1, seq, LANE), lambda b, h, i: (b, 0, P_DV // LANE + h)),
                  pl.BlockSpec((4, DIFF_DH), lambda b, h, i: (0, 0))],
        out_specs=pl.BlockSpec((1, ATT_TQ, LANE), lambda b, h, i: (b, i, h)),
        out_shape=jax.ShapeDtypeStruct((batch, seq, 512), BF16),
        scratch_shapes=[pltpu.VMEM((2, ATT_TQ, tk), F32), pltpu.VMEM((2, ATT_TQ, tk), F32)],
        compiler_params=_cparams(("parallel", "parallel", "arbitrary")),
        name="diffattn",
    )(qk, qk, p3, lam_p)


GPREP_TL = 512
HALO = 16


def _gdn_prep_kernel(x_ref, prev_ref, next_ref, w_ref, o_ref, xe_ref):
    j = pl.program_id(1)
    s = pl.program_id(2)
    nblk = pl.num_programs(1)
    tl = GPREP_TL
    prev = prev_ref[0].astype(F32)
    nxt = next_ref[0].astype(F32)
    xe_ref[0:HALO, :] = jnp.where(j == 0, 0.0, prev)
    xe_ref[HALO:HALO + tl, :] = x_ref[0].astype(F32)
    xe_ref[HALO + tl:2 * HALO + tl, :] = jnp.where(j == nblk - 1, 0.0, nxt)
    w = w_ref[...]
    y = (xe_ref[pl.ds(HALO - 1, tl), :] * w[0:1] + xe_ref[pl.ds(HALO, tl), :] * w[1:2]
         + xe_ref[pl.ds(HALO + 1, tl), :] * w[2:3] + xe_ref[pl.ds(HALO + 2, tl), :] * w[3:4])
    y = _silu(y)
    qscale = jnp.where(s == 0, GDN_DK ** -0.5, 1.0)
    for h in range(GDN_HEADS):
        cs = slice(h * GDN_DK, (h + 1) * GDN_DK)
        yh = y[:, cs]
        nrm = lax.rsqrt(jnp.sum(yh * yh, axis=-1, keepdims=True) + EPS) * qscale
        fac = jnp.where(s == 2, 1.0, nrm)
        o_ref[0, 0, :, cs] = (yh * fac).astype(o_ref.dtype)


def _gdn_prep(p3, conv_w, batch, seq):
    nblk = seq // GPREP_TL
    hb = GPREP_TL // HALO
    last = seq // HALO - 1
    base = P_NQ // 512
    return pl.pallas_call(
        _gdn_prep_kernel,
        grid=(batch, nblk, 3),
        in_specs=[pl.BlockSpec((1, GPREP_TL, 512), lambda b, j, s: (b, j, base + s)),
                  pl.BlockSpec((1, HALO, 512), lambda b, j, s: (b, jnp.maximum(j * hb - 1, 0), base + s)),
                  pl.BlockSpec((1, HALO, 512), lambda b, j, s: (b, jnp.minimum((j + 1) * hb, last), base + s)),
                  pl.BlockSpec((CONV_W, 512), lambda b, j, s: (0, s))],
        out_specs=pl.BlockSpec((1, 1, GPREP_TL, 512), lambda b, j, s: (s, b, j, 0)),
        out_shape=jax.ShapeDtypeStruct((3, batch, seq, 512), BF16),
        scratch_shapes=[pltpu.VMEM((GPREP_TL + 2 * HALO, 512), F32)],
        compiler_params=_cparams(("parallel", "parallel", "arbitrary")),
        name="gdn_prep",
    )(p3, p3, p3, conv_w)


GDN_TL = 512
GDN_CB = 4


def _gdn_kernel(q_ref, k_ref, v_ref, sm_ref, sel_ref, nega_ref, dtb_ref, o_ref, s_ref):
    d = pl.program_id(0)
    nc = GDN_TL // CHUNK

    @pl.when(pl.program_id(2) == 0)
    def _():
        s_ref[...] = jnp.zeros_like(s_ref)

    incl, strict = _tri_masks(d)
    tri = jnp.where(incl, 1.0, 0.0).astype(BF16)
    tri3 = jnp.concatenate([tri, tri, tri], axis=1)
    upper = jnp.where(strict, 1.0, 0.0)
    r_i = lax.broadcasted_iota(jnp.int32, (CHUNK, CHUNK), 0)
    c_i = lax.broadcasted_iota(jnp.int32, (CHUNK, CHUNK), 1)
    eye = jnp.where(r_i == c_i, 1.0, 0.0)
    fwd = d == 0
    H = GDN_HEADS

    def group(gi, carry):
        rows = []
        for ci in range(GDN_CB):
            c = gi * GDN_CB + ci
            ce = jnp.where(fwd, c, nc - 1 - c)
            rows.append(pl.ds(pl.multiple_of(ce * CHUNK, CHUNK), CHUNK))
        sel = [_dot(sm_ref[0, r, :], sel_ref[0]) for r in rows]
        beta = [_sigmoid(x) for x in sel]
        g = [nega_ref[0] * _softplus(x + dtb_ref[0]) for x in sel]
        g2 = [jnp.concatenate([x[:, H + h:H + h + 1] * upper for h in range(H)] + [x], axis=1) for x in g]
        dd = [_dot(tri3, _split3_rows(x)) for x in g2]
        ch = [(ci, h) for ci in range(GDN_CB) for h in range(H)]
        hs = [slice(h * GDN_DK, (h + 1) * GDN_DK) for h in range(H)]
        k_a = [k_ref[0, 0, rows[ci], hs[h]] for ci, h in ch]
        q_a = [q_ref[0, 0, rows[ci], hs[h]] for ci, h in ch]
        v_a = [v_ref[0, 0, rows[ci], hs[h]] for ci, h in ch]
        n_ch = range(len(ch))
        kf = [k_a[i].astype(F32) for i in n_ch]
        gam = [jnp.where(incl, jnp.exp(dd[ci][:, h * CHUNK:(h + 1) * CHUNK]), 0.0) for ci, h in ch]
        bcol = [dd[ci][:, H * CHUNK + H + h:H * CHUNK + H + h + 1] for ci, h in ch]
        blast = [jnp.where(fwd, b[CHUNK - 1:CHUNK], b[0:1]) for b in bcol]
        eb = [jnp.exp(b) for b in bcol]
        bet = [beta[ci][:, h:h + 1] for ci, h in ch]
        kk = [_dot_nt(k_a[i], k_a[i]) for i in n_ch]
        qk = [_dot_nt(q_a[i], k_a[i]) for i in n_ch]
        pw = [jnp.where(strict, -(bet[i] * kk[i] * gam[i]), 0.0) for i in n_ch]
        t = [eye + pw[i] for i in n_ch]
        for _ in range(5):
            pw = [_dot_split(pw[i], pw[i]) for i in n_ch]
            t = [t[i] + _dot_split(t[i], pw[i]) for i in n_ch]
        rhs = [jnp.concatenate([v_a[i].astype(F32) * bet[i], kf[i] * (bet[i] * eb[i])], axis=1).astype(BF16)
               for i in n_ch]
        sol = [_dot(t[i].astype(BF16), rhs[i]) for i in n_ch]
        u = [sol[i][:, :GDN_DV] for i in n_ch]
        w = [sol[i][:, GDN_DV:].astype(BF16) for i in n_ch]
        attn = [(qk[i] * gam[i]).astype(BF16) for i in n_ch]
        qdec = [(q_a[i].astype(F32) * eb[i]).astype(BF16) for i in n_ch]
        kend = [(kf[i] * jnp.exp(blast[i] - bcol[i])).astype(BF16) for i in n_ch]
        dec = [jnp.exp(b) for b in blast]
        st = [s_ref[h] for h in range(H)]
        for ci in range(GDN_CB):
            ix = [ci * H + h for h in range(H)]
            st_b = [st[h].astype(BF16) for h in range(H)]
            vnb = [(u[i] - _dot(w[i], st_b[h])).astype(BF16) for h, i in enumerate(ix)]
            o_new = [(_dot(qdec[i], st_b[h]) + _dot(attn[i], vnb[h])).astype(o_ref.dtype) for h, i in enumerate(ix)]
            st = [dec[i] * st[h] + _dot_tn(kend[i], vnb[h]) for h, i in enumerate(ix)]
            for h in range(H):
                o_ref[0, 0, rows[ci], hs[h]] = o_new[h]
        for h in range(H):
            s_ref[h] = st[h]
        return carry

    lax.fori_loop(0, nc // GDN_CB, group, 0)


def _gdn(qkv, p3, sel_m, nega, dtb, batch, seq):
    nblk = seq // GDN_TL

    def blk(dd, j):
        return j + dd * (nblk - 1 - 2 * j)

    def qspec(idx):
        return pl.BlockSpec((1, 1, GDN_TL, 512), lambda dd, b, j: (idx, b, blk(dd, j), 0))

    return pl.pallas_call(
        _gdn_kernel,
        grid=(2, batch, nblk),
        in_specs=[qspec(0), qspec(1), qspec(2),
                  pl.BlockSpec((1, GDN_TL, LANE), lambda dd, b, j: (b, blk(dd, j), P_SMALL // LANE)),
                  pl.BlockSpec((1, LANE, LANE), lambda dd, b, j: (dd, 0, 0)),
                  pl.BlockSpec((1, 1, LANE), lambda dd, b, j: (dd, 0, 0)),
                  pl.BlockSpec((1, 1, LANE), lambda dd, b, j: (dd, 0, 0))],
        out_specs=pl.BlockSpec((1, 1, GDN_TL, 512), lambda dd, b, j: (dd, b, blk(dd, j), 0)),
        out_shape=jax.ShapeDtypeStruct((2, batch, seq, 512), BF16),
        scratch_shapes=[pltpu.VMEM((GDN_HEADS, GDN_DK, GDN_DV), F32)],
        compiler_params=_cparams(("parallel", "parallel", "arbitrary")),
        name="gdn",
    )(qkv, qkv, qkv, p3, sel_m, nega, dtb)


MERGE_TM = 512


def _head_rms(o, w):
    parts = []
    for h in range(4):
        oh = o[:, h * LANE:(h + 1) * LANE]
        parts.append(oh * lax.rsqrt(jnp.mean(oh * oh, axis=-1, keepdims=True) + EPS))
    return jnp.concatenate(parts, axis=1) * w


def _merge_kernel(gf_ref, gb_ref, gr_ref, od_ref, nf_ref, nb_ref, nz_ref, mg0_ref, mg1_ref, mg2_ref,
                  x_ref, g1_ref, nrm_ref, wb_ref, wo_ref, nw_ref, o_ref, *, diff_scale):
    nrm = nrm_ref[...]
    y_gla = _head_rms(gf_ref[0, 0].astype(F32) + gb_ref[0, 0].astype(F32), nrm[0:1]) * _silu(gr_ref[0].astype(F32))
    y_diff = _head_rms(od_ref[0].astype(F32), nrm[1:2]) * diff_scale
    y_gdn = _head_rms(nf_ref[0, 0].astype(F32) + nb_ref[0, 0].astype(F32), nrm[2:3]) * _silu(nz_ref[0].astype(F32))
    merged = (_sigmoid(mg0_ref[0].astype(F32)) * _dot(y_gla.astype(BF16), wb_ref[0])
              + _sigmoid(mg1_ref[0].astype(F32)) * _dot(y_diff.astype(BF16), wb_ref[1])
              + _sigmoid(mg2_ref[0].astype(F32)) * _dot(y_gdn.astype(BF16), wb_ref[2]))
    y = _dot(merged.astype(BF16), wo_ref[...])
    o_ref[0] = x_ref[0] + g1_ref[0] * _rms_rows(y, nw_ref[...])


def _merge(gla_o, p3, att_o, gdn_o, x3, g1, nrm3, wb, wo, nw, diff_scale, batch, seq):
    tm = MERGE_TM
    nblk = seq // tm
    d = D_MODEL

    def dirspec(idx):
        return pl.BlockSpec((1, 1, tm, 512), lambda b, j: (idx, b, j, 0))

    def pspec(width, col):
        return pl.BlockSpec((1, tm, width), lambda b, j: (b, j, col // width))

    return pl.pallas_call(
        functools.partial(_merge_kernel, diff_scale=diff_scale),
        grid=(batch, nblk),
        in_specs=[dirspec(0), dirspec(1), pspec(512, P_GR),
                  pl.BlockSpec((1, tm, 512), lambda b, j: (b, j, 0)),
                  dirspec(0), dirspec(1), pspec(512, P_NZ),
                  pspec(d, P_MG), pspec(d, P_MG + d), pspec(d, P_MG + 2 * d),
                  pl.BlockSpec((1, tm, d), lambda b, j: (b, j, 0)),
                  pl.BlockSpec((1, 1, d), lambda b, j: (b, 0, 0)),
                  pl.BlockSpec((3, 512), lambda b, j: (0, 0)),
                  pl.BlockSpec((3, 512, d), lambda b, j: (0, 0, 0)),
                  pl.BlockSpec((d, d), lambda b, j: (0, 0)),
                  pl.BlockSpec((1, d), lambda b, j: (0, 0))],
        out_specs=pl.BlockSpec((1, tm, d), lambda b, j: (b, j, 0)),
        out_shape=jax.ShapeDtypeStruct((batch, seq, d), F32),
        compiler_params=_cparams(("parallel", "parallel")),
        name="merge",
    )(gla_o, gla_o, p3, att_o, gdn_o, gdn_o, p3, p3, p3, p3, x3, g1, nrm3, wb, wo, nw)


FFN_TM = 512


def _ffn_kernel(x_ref, nw_ref, sc_ref, sh_ref, w1_ref, w3_ref, w2_ref, g2_ref, nwo_ref, o_ref, h_ref, acc_ref):
    j = pl.program_id(1)

    @pl.when(j == 0)
    def _():
        y = _rms_rows(x_ref[...], nw_ref[...])
        h_ref[...] = (y * (1.0 + sc_ref[0]) + sh_ref[0]).astype(BF16)
        acc_ref[...] = jnp.zeros_like(acc_ref)

    h = h_ref[...]
    a = _dot(h, w1_ref[...])
    b = _dot(h, w3_ref[...])
    acc_ref[...] += _dot((_silu(a) * b).astype(BF16), w2_ref[...])

    @pl.when(j == pl.num_programs(1) - 1)
    def _():
        o_ref[...] = x_ref[...] + g2_ref[0] * _rms_rows(acc_ref[...], nwo_ref[...])


def _ffn(x2, nw, sc, sh, w1, w3, w2, g2, nwo, seq):
    m, d = x2.shape
    f = w1.shape[1]
    tm, tf = FFN_TM, f // 2
    per_b = seq // tm
    vec = pl.BlockSpec((1, d), lambda i, j: (0, 0))
    mod = pl.BlockSpec((1, 1, d), lambda i, j: (i // per_b, 0, 0))
    return pl.pallas_call(
        _ffn_kernel,
        grid=(m // tm, f // tf),
        in_specs=[pl.BlockSpec((tm, d), lambda i, j: (i, 0)), vec, mod, mod,
                  pl.BlockSpec((d, tf), lambda i, j: (0, j)),
                  pl.BlockSpec((d, tf), lambda i, j: (0, j)),
                  pl.BlockSpec((tf, d), lambda i, j: (j, 0)),
                  mod, vec],
        out_specs=pl.BlockSpec((tm, d), lambda i, j: (i, 0)),
        out_shape=jax.ShapeDtypeStruct((m, d), F32),
        scratch_shapes=[pltpu.VMEM((tm, d), BF16), pltpu.VMEM((tm, d), F32)],
        compiler_params=_cparams(("parallel", "arbitrary")),
        name="ffn",
    )(x2, nw, sc, sh, w1, w3, w2, g2, nwo)


ROUTER_TM = 1024


def _router_kernel(x_ref, nw_ref, sc_ref, sh_ref, rw_ref, o_ref, h_ref):
    y = _rms_rows(x_ref[...], nw_ref[...])
    h = y * (1.0 + sc_ref[0]) + sh_ref[0]
    h_ref[...] = h
    logits = jnp.dot(h, rw_ref[...], preferred_element_type=F32, precision=lax.Precision.HIGHEST)
    lane = lax.broadcasted_iota(jnp.int32, logits.shape, 1)
    neg = -jnp.inf
    logits = jnp.where(lane < N_EXPERTS, logits, neg)
    m1 = jnp.max(logits, axis=-1, keepdims=True)
    i1 = jnp.min(jnp.where(logits == m1, lane, LANE), axis=-1, keepdims=True)
    rest = jnp.where(lane == i1, neg, logits)
    m2 = jnp.max(rest, axis=-1, keepdims=True)
    i2 = jnp.min(jnp.where(rest == m2, lane, LANE), axis=-1, keepdims=True)
    e = jnp.exp(m2 - m1)
    w1 = 1.0 / (1.0 + e)
    w2 = e / (1.0 + e)
    o_ref[...] = (jnp.where(lane == 0, i1.astype(F32), 0.0) + jnp.where(lane == 1, i2.astype(F32), 0.0)
                  + jnp.where(lane == 2, w1, 0.0) + jnp.where(lane == 3, w2, 0.0))


def _router(x2, nw, sc, sh, rw_pad, seq):
    m, d = x2.shape
    tm = ROUTER_TM
    per_b = seq // tm
    return pl.pallas_call(
        _router_kernel,
        grid=(m // tm,),
        in_specs=[pl.BlockSpec((tm, d), lambda i: (i, 0)),
                  pl.BlockSpec((1, d), lambda i: (0, 0)),
                  pl.BlockSpec((1, 1, d), lambda i: (i // per_b, 0, 0)),
                  pl.BlockSpec((1, 1, d), lambda i: (i // per_b, 0, 0)),
                  pl.BlockSpec((d, LANE), lambda i: (0, 0))],
        out_specs=[pl.BlockSpec((tm, LANE), lambda i: (i, 0)), pl.BlockSpec((tm, d), lambda i: (i, 0))],
        out_shape=[jax.ShapeDtypeStruct((m, LANE), F32), jax.ShapeDtypeStruct((m, d), F32)],
        compiler_params=_cparams(("parallel",)),
        name="router",
    )(x2, nw, sc, sh, rw_pad)


MOE_TM = 448
MOE_NF = 7


def _moe_plan(rt, m):
    tm = MOE_TM
    n_assign = TOP_K * m
    nt = -(-n_assign // tm) + N_EXPERTS
    e_all = rt[:, :TOP_K].astype(jnp.int32).reshape(-1)
    onehot = (e_all[:, None] == jnp.arange(N_EXPERTS, dtype=jnp.int32)[None, :]).astype(jnp.int32)
    csum = jnp.cumsum(onehot, axis=0)
    rank = jnp.take_along_axis(csum, e_all[:, None], axis=1)[:, 0] - 1
    counts = csum[-1]
    padded = ((counts + tm - 1) // tm) * tm
    ends = jnp.cumsum(padded)
    dest = (ends - padded)[e_all] + rank
    src = jnp.full((nt * tm,), -1, jnp.int32).at[dest].set(jnp.arange(n_assign, dtype=jnp.int32))
    tile_expert = jnp.minimum(jnp.searchsorted(ends, jnp.arange(nt, dtype=jnp.int32) * tm, side="right"),
                              N_EXPERTS - 1).astype(jnp.int32)
    src = src.reshape(nt, tm)
    valid = src >= 0
    gather_tok = jnp.where(valid, src // TOP_K, 0)
    dump_row = m + jnp.arange(tm, dtype=jnp.int32)[None, :]
    tile_par = (jnp.arange(nt, dtype=jnp.int32) % 2)[:, None]
    srow = jnp.concatenate([dump_row, jnp.where(valid, src // TOP_K, dump_row)], axis=0)
    scol = jnp.concatenate([jnp.ones((1, tm), jnp.int32), jnp.where(valid, src % TOP_K, tile_par)], axis=0)
    return (gather_tok.reshape(nt, 1, tm), srow.reshape(nt + 1, 1, tm), scol.reshape(nt + 1, 1, tm), tile_expert)


def _moe_kernel(te_ref, srow_ref, scol_ref, cur_ref, nxt_ref, h_hbm, w1_ref, w3_ref, w2_ref, y_hbm,
                gbuf, obuf, hb_ref, acc_ref, gsem, ssem):
    t = pl.program_id(0)
    j = pl.program_id(1)
    nt = pl.num_programs(0) - 1
    tm = MOE_TM
    d = hb_ref.shape[1]
    per_step = tm // MOE_NF
    slot = t % 2

    def gather_copy(idx_ref, r, sl):
        return pltpu.make_async_copy(h_hbm.at[pl.ds(idx_ref[0, 0, r], 1)], gbuf.at[sl, pl.ds(r, 1)], gsem.at[sl])

    def scatter_copy(r, sl):
        col = pl.multiple_of(scol_ref[0, 0, r] * d, d)
        return pltpu.make_async_copy(obuf.at[sl, pl.ds(r, 1)], y_hbm.at[pl.ds(srow_ref[0, 0, r], 1), pl.ds(col, d)],
                                     ssem.at[sl])

    def wait_gather(sl):
        pltpu.make_async_copy(gbuf.at[sl], gbuf.at[sl], gsem.at[sl]).wait()

    def wait_scatter(sl):
        pltpu.make_async_copy(obuf.at[sl], obuf.at[sl], ssem.at[sl]).wait()

    @pl.when((t == 0) & (j == 0))
    def _():
        def body(r, c):
            gather_copy(cur_ref, r, 0).start()
            return c
        lax.fori_loop(0, tm, body, 0)
        obuf[1] = jnp.zeros((tm, d), F32)

    @pl.when(j == 0)
    def _():
        wait_gather(slot)
        hb_ref[...] = gbuf[slot].astype(BF16)
        acc_ref[...] = jnp.zeros_like(acc_ref)

    base = j * per_step
    for i in range(per_step):
        gather_copy(nxt_ref, base + i, 1 - slot).start()
        scatter_copy(base + i, 1 - slot).start()
    h = hb_ref[...]
    a = _dot(h, w1_ref[0])
    b = _dot(h, w3_ref[0])
    acc_ref[...] += _dot((_silu(a) * b).astype(BF16), w2_ref[0])

    @pl.when(j == MOE_NF - 1)
    def _():
        @pl.when(t > 0)
        def _():
            wait_scatter(slot)

        obuf[slot] = acc_ref[...]

        @pl.when(t == nt)
        def _():
            wait_scatter(1 - slot)
            wait_gather(1 - slot)


def _moe(h, gather_tok, srow, scol, tile_expert, w1, w3, w2):
    m, d = h.shape
    ne, _, f = w1.shape
    tm, tf = MOE_TM, f // MOE_NF
    nt = gather_tok.shape[0]

    def smem_tile(idx):
        return pl.BlockSpec((1, 1, tm), idx, memory_space=pltpu.SMEM)

    def tile(t):
        return jnp.minimum(t, nt - 1)

    grid_spec = pltpu.PrefetchScalarGridSpec(
        num_scalar_prefetch=1,
        grid=(nt + 1, MOE_NF),
        in_specs=[smem_tile(lambda t, j, te: (t, 0, 0)),
                  smem_tile(lambda t, j, te: (t, 0, 0)),
                  smem_tile(lambda t, j, te: (tile(t), 0, 0)),
                  smem_tile(lambda t, j, te: (tile(t + 1), 0, 0)),
                  pl.BlockSpec(memory_space=pl.ANY),
                  pl.BlockSpec((1, d, tf), lambda t, j, te: (te[tile(t)], 0, j)),
                  pl.BlockSpec((1, d, tf), lambda t, j, te: (te[tile(t)], 0, j)),
                  pl.BlockSpec((1, tf, d), lambda t, j, te: (te[tile(t)], j, 0))],
        out_specs=pl.BlockSpec(memory_space=pl.ANY),
        scratch_shapes=[pltpu.VMEM((2, tm, d), F32), pltpu.VMEM((2, tm, d), F32), pltpu.VMEM((tm, d), BF16),
                        pltpu.VMEM((tm, d), F32), pltpu.SemaphoreType.DMA((2,)), pltpu.SemaphoreType.DMA((2,))])
    return pl.pallas_call(
        _moe_kernel,
        grid_spec=grid_spec,
        out_shape=jax.ShapeDtypeStruct((m + tm, TOP_K * d), F32),
        compiler_params=_cparams(("arbitrary", "arbitrary")),
        name="moe",
    )(tile_expert, srow, scol, gather_tok, gather_tok, h, w1, w3, w2)


COMBINE_TM = 512


def _combine_kernel(y_ref, rt_ref, x_ref, g2_ref, nwo_ref, o_ref):
    d = x_ref.shape[1]
    rt = rt_ref[...]
    y = rt[:, 2:3] * y_ref[:, :d] + rt[:, 3:4] * y_ref[:, d:]
    o_ref[...] = x_ref[...] + g2_ref[0] * _rms_rows(y, nwo_ref[...])


def _combine(y2, rt, x2, g2, nwo, seq):
    m, d = x2.shape
    tm = COMBINE_TM
    per_b = seq // tm
    return pl.pallas_call(
        _combine_kernel,
        grid=(m // tm,),
        in_specs=[pl.BlockSpec((tm, TOP_K * d), lambda i: (i, 0)),
                  pl.BlockSpec((tm, LANE), lambda i: (i, 0)),
                  pl.BlockSpec((tm, d), lambda i: (i, 0)),
                  pl.BlockSpec((1, 1, d), lambda i: (i // per_b, 0, 0)),
                  pl.BlockSpec((1, d), lambda i: (0, 0))],
        out_specs=pl.BlockSpec((tm, d), lambda i: (i, 0)),
        out_shape=jax.ShapeDtypeStruct((m, d), F32),
        compiler_params=_cparams(("parallel",)),
        name="moe_combine",
    )(y2, rt, x2, g2, nwo)


def _reorder_w_in(w):
    o = _IN_OFF

    def cols(a, b):
        return w[:, o[a]:o[b]]

    small = jnp.concatenate([cols(4, 6), cols(13, 17)], axis=1)
    pad = jnp.zeros((w.shape[0], P_COLS - P_SMALL - small.shape[1]), w.dtype)
    out = jnp.concatenate([cols(6, 9),
                           cols(2, 4),
                           cols(0, 2),
                           cols(9, 13),
                           cols(17, 18),
                           small, pad], axis=1)
    return out.astype(BF16)


def _gla_gate_params(up, bias):
    hi = up.astype(BF16)
    lo = (up - hi.astype(F32)).astype(BF16)
    out = jnp.zeros((2, 2 * LANE, GLA_HEADS * GLA_DK), BF16)
    for dd, off in ((0, S_GWF), (1, S_GWB)):
        out = out.at[dd, off:off + GLA_RANK].set(hi[dd])
        out = out.at[dd, LANE + off:LANE + off + GLA_RANK].set(lo[dd])
    return out, bias.reshape(2, 1, -1).astype(F32)


def _gdn_gate_params(a_log, dt_bias):
    h = GDN_HEADS
    sel = np.zeros((2, LANE, LANE), np.float32)
    for dd, (boff, aoff) in enumerate(((S_NBF, S_NAF), (S_NBB, S_NAB))):
        for i in range(h):
            sel[dd, boff + i, i] = 1.0
            sel[dd, aoff + i, h + i] = 1.0
    nega = jnp.zeros((2, 1, LANE), F32).at[:, 0, h:2 * h].set(-jnp.exp(a_log.astype(F32)))
    dtb = jnp.zeros((2, 1, LANE), F32).at[:, 0, h:2 * h].set(dt_bias.astype(F32))
    return jnp.asarray(sel, BF16), nega, dtb


def _rope_inv_row():
    inv = ROPE_THETA ** (-jnp.arange(0, ROT_DIM, 2, dtype=F32) / ROT_DIM)
    lane = np.arange(LANE) % DIFF_DH
    row = jnp.where(jnp.asarray(lane < ROT_DIM), inv[jnp.asarray(lane % (ROT_DIM // 2))], 0.0)
    return row.reshape(1, LANE).astype(F32)


def kernel(x, c, positions, adaln_w, adaln_b, norm_w, w_in, gla_gate_up, gla_gate_bias, gla_norm, diff_lambda,
           diff_norm, gdn_conv, gdn_A_log, gdn_dt_bias, gdn_norm, w_branch, w_out, ffn_w1, ffn_w3, ffn_w2,
           router_w, moe_w1, moe_w3, moe_w2):
    batch, seq, d = x.shape
    depth = adaln_w.shape[0]
    m = batch * seq

    c_pad = jnp.zeros((8, d), F32).at[:batch].set(c)
    mod = _adaln(c_pad, adaln_w, adaln_b)[:, :batch]
    pos3 = positions.reshape(batch, seq, 1)
    inv_row = _rope_inv_row()

    for layer in range(depth):
        sh1, sc1, g1, sh2, sc2, g2 = [t.reshape(batch, 1, d) for t in jnp.split(mod[layer], 6, axis=-1)]
        nw = norm_w[layer].astype(F32)

        p = _inproj(x.reshape(m, d), nw[0:1], sc1, sh1, _reorder_w_in(w_in[layer]), seq)
        p3 = p.reshape(batch, seq, P_COLS)

        up2, gbias = _gla_gate_params(gla_gate_up[layer], gla_gate_bias[layer])
        gla_o = _gla(p3, up2, gbias, batch, seq)

        qk = _rope(p3, pos3, inv_row, batch, seq)
        lam_init = 0.8 - 0.6 * math.exp(-0.3 * layer)
        att_o = _attn(qk, p3, diff_lambda[layer].astype(F32), lam_init, batch, seq)

        qkv = _gdn_prep(p3, gdn_conv[layer].astype(F32), batch, seq)
        sel_m, nega, dtb = _gdn_gate_params(gdn_A_log[layer], gdn_dt_bias[layer])
        gdn_o = _gdn(qkv, p3, sel_m, nega, dtb, batch, seq)

        nrm3 = jnp.stack([jnp.tile(gla_norm[layer], 4), jnp.tile(diff_norm[layer], 4),
                          jnp.tile(gdn_norm[layer], 4)]).astype(F32)
        x = _merge(gla_o, p3, att_o, gdn_o, x, g1, nrm3, w_branch[layer].astype(BF16),
                   w_out[layer].astype(BF16), nw[1:2], 1.0 - lam_init, batch, seq)

        x2 = x.reshape(m, d)
        if layer % 2 == 0:
            li = layer // 2
            x2 = _ffn(x2, nw[2:3], sc2, sh2, ffn_w1[li].astype(BF16), ffn_w3[li].astype(BF16),
                      ffn_w2[li].astype(BF16), g2, nw[3:4], seq)
        else:
            li = layer // 2
            rw_pad = jnp.zeros((d, LANE), F32).at[:, :N_EXPERTS].set(router_w[li])
            rt, h2 = _router(x2, nw[2:3], sc2, sh2, rw_pad, seq)
            y = _moe(h2, *_moe_plan(rt, m), moe_w1[li].astype(BF16), moe_w3[li].astype(BF16),
                     moe_w2[li].astype(BF16))
            x2 = _combine(y, rt, x2, g2, nw[3:4], seq)
        x = x2.reshape(batch, seq, d)
    return x
```

```python
import functools
import math

import jax
import jax.numpy as jnp
import numpy as np
from jax import lax
from jax.experimental import pallas as pl
from jax.experimental.pallas import tpu as pltpu

F32 = jnp.float32
BF16 = jnp.bfloat16

D_MODEL = 1024
GLA_HEADS, GLA_DK, GLA_DV, GLA_RANK, GLA_TAU = 4, 64, 128, 16, 16.0
DIFF_HEADS, DIFF_DH = 4, 64
ROT_DIM = DIFF_DH // 4
ROPE_THETA = 500000.0
GDN_HEADS, GDN_DK, GDN_DV, CONV_W = 4, 128, 128, 4
CHUNK = 64
N_BRANCH = 3
N_EXPERTS, TOP_K = 8, 2
EPS = 1e-6
BRANCH_W = 512

_IN_SIZES = (256, 256, 512, 512, 16, 16, 512, 512, 512, 512, 512, 512, 512, 4, 4, 4, 4, 3 * D_MODEL)
_IN_OFF = tuple(int(v) for v in np.concatenate([[0], np.cumsum(_IN_SIZES)]))

P_DQ, P_DK, P_DV = 0, 512, 1024
P_GV, P_GR = 1536, 2048
P_GQ, P_GK = 2560, 2816
P_NQ, P_NK, P_NV, P_NZ = 3072, 3584, 4096, 4608
P_MG = 5120
P_SMALL = 8192
P_COLS = 8448
S_GWF, S_GWB, S_NBF, S_NBB, S_NAF, S_NAB = 0, 16, 32, 36, 40, 44

VMEM_LIMIT = 56 * 1024 * 1024
LANE = 128


def _cparams(sem):
    return pltpu.CompilerParams(dimension_semantics=sem, vmem_limit_bytes=VMEM_LIMIT)


def _dot(a, b):
    return jnp.dot(a, b, preferred_element_type=F32)


def _dot_nt(a, b):
    return lax.dot_general(a, b, (((1,), (1,)), ((), ())), preferred_element_type=F32)


def _dot_tn(a, b):
    return lax.dot_general(a, b, (((0,), (0,)), ((), ())), preferred_element_type=F32)


def _split3_rows(x):
    hi = x.astype(BF16)
    r1 = x - hi.astype(F32)
    mid = r1.astype(BF16)
    lo = (r1 - mid.astype(F32)).astype(BF16)
    return jnp.concatenate([hi, mid, lo], axis=0)


def _dot_split(a, b):
    a_hi = a.astype(BF16)
    a_lo = (a - a_hi.astype(F32)).astype(BF16)
    b_hi = b.astype(BF16)
    b_lo = (b - b_hi.astype(F32)).astype(BF16)
    return _dot(jnp.concatenate([a_hi, a_lo, a_hi], axis=1), jnp.concatenate([b_hi, b_hi, b_lo], axis=0))


def _sigmoid(x):
    return 1.0 / (1.0 + jnp.exp(-x))


def _silu(x):
    return x * _sigmoid(x)


def _softplus(x):
    return jnp.maximum(x, 0.0) + jnp.log1p(jnp.exp(-jnp.abs(x)))


def _rms_rows(x, w):
    return x * lax.rsqrt(jnp.mean(x * x, axis=-1, keepdims=True) + EPS) * w


def _tri_masks(d):
    r = lax.broadcasted_iota(jnp.int32, (CHUNK, CHUNK), 0)
    c = lax.broadcasted_iota(jnp.int32, (CHUNK, CHUNK), 1)
    lag = (r - c) * (1 - 2 * d)
    return lag >= 0, lag > 0


def _adaln_kernel(c_ref, w_ref, b_ref, o_ref):
    c = c_ref[...]
    o_ref[0] = jnp.dot(_silu(c), w_ref[0], preferred_element_type=F32,
                       precision=lax.Precision.HIGHEST) + b_ref[0]


def _adaln(c_pad, adaln_w, adaln_b):
    depth, d, n = adaln_w.shape
    tn = 1536
    return pl.pallas_call(
        _adaln_kernel,
        grid=(depth, n // tn),
        in_specs=[pl.BlockSpec((8, d), lambda l, j: (0, 0)),
                  pl.BlockSpec((1, d, tn), lambda l, j: (l, 0, j)),
                  pl.BlockSpec((1, 1, tn), lambda l, j: (l, 0, j))],
        out_specs=pl.BlockSpec((1, 8, tn), lambda l, j: (l, 0, j)),
        out_shape=jax.ShapeDtypeStruct((depth, 8, n), F32),
        compiler_params=_cparams(("parallel", "parallel")),
        name="adaln",
    )(c_pad, adaln_w, adaln_b.reshape(depth, 1, n))


def _inproj_kernel(x_ref, nw_ref, sc_ref, sh_ref, w_ref, o_ref, h_ref):
    @pl.when(pl.program_id(1) == 0)
    def _():
        y = _rms_rows(x_ref[...], nw_ref[...])
        h_ref[...] = (y * (1.0 + sc_ref[0]) + sh_ref[0]).astype(BF16)

    o_ref[...] = _dot(h_ref[...], w_ref[...]).astype(o_ref.dtype)


def _inproj(x2, nw, sc, sh, w, seq):
    m, d = x2.shape
    n = w.shape[1]
    tm, tn = 1024, n // 3
    per_b = seq // tm
    return pl.pallas_call(
        _inproj_kernel,
        grid=(m // tm, n // tn),
        in_specs=[pl.BlockSpec((tm, d), lambda i, j: (i, 0)),
                  pl.BlockSpec((1, d), lambda i, j: (0, 0)),
                  pl.BlockSpec((1, 1, d), lambda i, j: (i // per_b, 0, 0)),
                  pl.BlockSpec((1, 1, d), lambda i, j: (i // per_b, 0, 0)),
                  pl.BlockSpec((d, tn), lambda i, j: (0, j))],
        out_specs=pl.BlockSpec((tm, tn), lambda i, j: (i, j)),
        out_shape=jax.ShapeDtypeStruct((m, n), BF16),
        scratch_shapes=[pltpu.VMEM((tm, d), BF16)],
        compiler_params=_cparams(("parallel", "arbitrary")),
        name="inproj",
    )(x2, nw, sc, sh, w)


GLA_TL = 512
GLA_CB = 8


def _gla_kernel(q_ref, k_ref, v_ref, sm_ref, up_ref, bias_ref, o_ref, st_ref):
    d = pl.program_id(0)
    nc = GLA_TL // CHUNK

    @pl.when(pl.program_id(2) == 0)
    def _():
        st_ref[...] = jnp.zeros_like(st_ref)

    incl, _ = _tri_masks(d)
    tri = jnp.where(incl, 1.0, 0.0).astype(BF16)
    tri3 = jnp.concatenate([tri, tri, tri], axis=1)
    fwd = d == 0
    scale = GLA_DK ** -0.5

    H = GLA_HEADS
    ks = [slice(h * GLA_DK, (h + 1) * GLA_DK) for h in range(H)]
    vs = [slice(h * GLA_DV, (h + 1) * GLA_DV) for h in range(H)]

    def group(gi, carry):
        rows = []
        for ci in range(GLA_CB):
            c = gi * GLA_CB + ci
            ce = jnp.where(fwd, c, nc - 1 - c)
            rows.append(pl.ds(pl.multiple_of(ce * CHUNK, CHUNK), CHUNK))
        cr = range(GLA_CB)
        sm = [sm_ref[0, r, :] for r in rows]
        z = [_dot(jnp.concatenate([x, x], axis=1), up_ref[0]) + bias_ref[0] for x in sm]
        g = [(jnp.minimum(x, 0.0) - jnp.log1p(jnp.exp(-jnp.abs(x)))) * (1.0 / GLA_TAU) for x in z]
        b = [_dot(tri3, _split3_rows(x)) for x in g]
        b_last = [jnp.where(fwd, x[CHUNK - 1:CHUNK], x[0:1]) for x in b]
        b_mid = [jnp.where(fwd, x[CHUNK // 2 - 1:CHUNK // 2], x[CHUNK // 2:CHUNK // 2 + 1]) for x in b]
        q = [q_ref[0, r, :].astype(F32) * scale for r in rows]
        k = [k_ref[0, r, :].astype(F32) for r in rows]
        qe = [(q[i] * jnp.exp(b[i] - b_mid[i])).astype(BF16) for i in cr]
        ke = [(k[i] * jnp.exp(b_mid[i] - b[i])).astype(BF16) for i in cr]
        kend = [(k[i] * jnp.exp(b_last[i] - b[i])).astype(BF16) for i in cr]
        qd = [(q[i] * jnp.exp(b[i])).astype(BF16) for i in cr]
        dec = [jnp.exp(x) for x in b_last]
        ch = [(ci, h) for ci in cr for h in range(H)]
        v_a = [v_ref[0, rows[ci], vs[h]] for ci, h in ch]
        a = [jnp.where(incl, _dot_nt(qe[ci][:, ks[h]], ke[ci][:, ks[h]]), 0.0).astype(BF16) for ci, h in ch]
        o_in = [_dot(a[i], v_a[i]) for i in range(len(ch))]
        kv = [_dot_tn(v_a[i], kend[ci][:, ks[h]]) for i, (ci, h) in enumerate(ch)]
        st = st_ref[...]
        for ci in cr:
            st_b = st.astype(BF16)
            for h in range(H):
                o = o_in[ci * H + h] + _dot_nt(qd[ci][:, ks[h]], st_b[:, ks[h]])
                o_ref[0, 0, rows[ci], vs[h]] = o.astype(o_ref.dtype)
            st = st * dec[ci] + jnp.concatenate([kv[ci * H + h] for h in range(H)], axis=1)
        st_ref[...] = st
        return carry

    lax.fori_loop(0, nc // GLA_CB, group, 0)


def _gla(p3, up2, bias, batch, seq):
    nblk = seq // GLA_TL

    def blk(dd, j):
        return j + dd * (nblk - 1 - 2 * j)

    return pl.pallas_call(
        _gla_kernel,
        grid=(2, batch, nblk),
        in_specs=[pl.BlockSpec((1, GLA_TL, 256), lambda dd, b, j: (b, blk(dd, j), P_GQ // 256)),
                  pl.BlockSpec((1, GLA_TL, 256), lambda dd, b, j: (b, blk(dd, j), P_GK // 256)),
                  pl.BlockSpec((1, GLA_TL, 512), lambda dd, b, j: (b, blk(dd, j), P_GV // 512)),
                  pl.BlockSpec((1, GLA_TL, LANE), lambda dd, b, j: (b, blk(dd, j), P_SMALL // LANE)),
                  pl.BlockSpec((1, 2 * LANE, 256), lambda dd, b, j: (dd, 0, 0)),
                  pl.BlockSpec((1, 1, 256), lambda dd, b, j: (dd, 0, 0))],
        out_specs=pl.BlockSpec((1, 1, GLA_TL, 512), lambda dd, b, j: (dd, b, blk(dd, j), 0)),
        out_shape=jax.ShapeDtypeStruct((2, batch, seq, 512), BF16),
        scratch_shapes=[pltpu.VMEM((GLA_DV, GLA_HEADS * GLA_DK), F32)],
        compiler_params=_cparams(("parallel", "parallel", "arbitrary")),
        name="gla",
    )(p3, p3, p3, p3, up2, bias)


ROPE_TL = 512


def _rope_kernel(q_ref, k_ref, pos_ref, inv_ref, o_ref):
    ang = pos_ref[0].astype(F32) * inv_ref[...]
    cosf = jnp.cos(ang)
    sinf = jnp.sin(ang)
    lane = lax.broadcasted_iota(jnp.int32, (1, LANE), 1) % DIFF_DH
    half = ROT_DIM // 2
    cmul = jnp.where(lane < ROT_DIM, cosf, 1.0)
    s_up = jnp.where(lane < half, -sinf, 0.0)
    s_dn = jnp.where((lane >= half) & (lane < ROT_DIM), sinf, 0.0)
    for s, (x_ref, scale) in enumerate(((q_ref, DIFF_DH ** -0.5 * math.log2(math.e)), (k_ref, 1.0))):
        for gidx in range(4):
            cs = slice(gidx * LANE, (gidx + 1) * LANE)
            x = x_ref[0, :, cs].astype(F32)
            y = x * cmul + pltpu.roll(x, LANE - half, 1) * s_up + pltpu.roll(x, half, 1) * s_dn
            o_ref[s, 0, :, cs] = (y * scale).astype(o_ref.dtype)


def _rope(p3, pos3, inv_row, batch, seq):
    nblk = seq // ROPE_TL
    return pl.pallas_call(
        _rope_kernel,
        grid=(batch, nblk),
        in_specs=[pl.BlockSpec((1, ROPE_TL, 512), lambda b, j: (b, j, P_DQ // 512)),
                  pl.BlockSpec((1, ROPE_TL, 512), lambda b, j: (b, j, P_DK // 512)),
                  pl.BlockSpec((1, ROPE_TL, 1), lambda b, j: (b, j, 0)),
                  pl.BlockSpec((1, LANE), lambda b, j: (0, 0))],
        out_specs=pl.BlockSpec((2, 1, ROPE_TL, 512), lambda b, j: (0, b, j, 0)),
        out_shape=jax.ShapeDtypeStruct((2, batch, seq, 512), BF16),
        compiler_params=_cparams(("parallel", "parallel")),
        name="rope",
    )(p3, p3, pos3, inv_row)


ATT_TQ = 512
ATT_TK = 2048


def _attn_kernel(q_ref, k_ref, v_ref, lam_ref, o_ref, sa_ref, sb_ref, *, lam_init):
    q = q_ref[0, 0]
    lane = lax.broadcasted_iota(jnp.int32, q.shape, 1)
    zero = jnp.zeros_like(q)
    q1 = jnp.where(lane < DIFF_DH, q, zero)
    q2 = jnp.where(lane >= DIFF_DH, q, zero)
    tq = q.shape[0]
    tk = sa_ref.shape[2]
    nk = k_ref.shape[2] // tk
    ones_col = jnp.where(lax.broadcasted_iota(jnp.int32, (tk, LANE), 1) == 0, 1.0, 0.0).astype(BF16)

    def soft(s, v1, m, l, acc):
        m_new = jnp.maximum(m, jnp.max(s, axis=-1, keepdims=True))
        p = jnp.exp2((s - m_new).astype(BF16))
        alpha = jnp.exp2(m - m_new)
        pv = _dot(p, v1)
        return m_new, alpha * l + pv[:, LANE:LANE + 1], alpha * acc + pv[:, :LANE]

    def k_tile(t):
        return k_ref[0, 0, pl.ds(t * tk, tk), :]

    def v_tile(t):
        return jnp.concatenate([v_ref[0, pl.ds(t * tk, tk), :], ones_col], axis=1)

    def scores(t, dst):
        k = k_tile(t)
        dst[0] = _dot_nt(q1, k)
        dst[1] = _dot_nt(q2, k)

    def consume(t, src, carry):
        m1, l1, a1, m2, l2, a2 = carry
        v1 = v_tile(t)
        m1, l1, a1 = soft(src[0], v1, m1, l1, a1)
        m2, l2, a2 = soft(src[1], v1, m2, l2, a2)
        return m1, l1, a1, m2, l2, a2

    neg = jnp.full((tq, 1), -jnp.inf, F32)
    zl = jnp.zeros((tq, 1), F32)
    za = jnp.zeros((tq, LANE), F32)
    carry = (neg, zl, za, neg, zl, za)
    bufs = (sa_ref, sb_ref)
    scores(0, bufs[0])
    for t in range(nk):
        if t + 1 < nk:
            scores(t + 1, bufs[(t + 1) % 2])
        carry = consume(t, bufs[t % 2], carry)
    m1, l1, a1, m2, l2, a2 = carry
    lp = lam_ref[...]
    lam = (jnp.exp(jnp.sum(lp[0:1] * lp[1:2], axis=-1, keepdims=True))
           - jnp.exp(jnp.sum(lp[2:3] * lp[3:4], axis=-1, keepdims=True)) + lam_init)
    o_ref[0] = (a1 / l1 - lam * (a2 / l2)).astype(o_ref.dtype)


def _attn(qk, p3, lam_p, lam_init, batch, seq):
    nq = seq // ATT_TQ
    tk = min(ATT_TK, seq // 2)
    return pl.pallas_call(
        functools.partial(_attn_kernel, lam_init=lam_init),
        grid=(batch, DIFF_HEADS, nq),
        in_specs=[pl.BlockSpec((1, 1, ATT_TQ, LANE), lambda b, h, i: (0, b, i, h)),
                  pl.BlockSpec((1, 1, seq, LANE), lambda b, h, i: (1, b, 0, h)),
                  pl.BlockSpec((1, seq, LANE), lambda b, h, i: (b, 0, P_DV // LANE + h)),
                  pl.BlockSpec((4, DIFF_DH), lambda b, h, i: (0, 0))],
        out_specs=pl.BlockSpec((1, ATT_TQ, LANE), lambda b, h, i: (b, i, h)),
        out_shape=jax.ShapeDtypeStruct((batch, seq, 512), BF16),
        scratch_shapes=[pltpu.VMEM((2, ATT_TQ, tk), F32), pltpu.VMEM((2, ATT_TQ, tk), F32)],
        compiler_params=_cparams(("parallel", "parallel", "arbitrary")),
        name="diffattn",
    )(qk, qk, p3, lam_p)


GPREP_TL = 512
HALO = 16


def _gdn_prep_kernel(x_ref, prev_ref, next_ref, w_ref, o_ref, xe_ref):
    j = pl.program_id(1)
    s = pl.program_id(2)
    nblk = pl.num_programs(1)
    tl = GPREP_TL
    prev = prev_ref[0].astype(F32)
    nxt = next_ref[0].astype(F32)
    xe_ref[0:HALO, :] = jnp.where(j == 0, 0.0, prev)
    xe_ref[HALO:HALO + tl, :] = x_ref[0].astype(F32)
    xe_ref[HALO + tl:2 * HALO + tl, :] = jnp.where(j == nblk - 1, 0.0, nxt)
    w = w_ref[...]
    y = (xe_ref[pl.ds(HALO - 1, tl), :] * w[0:1] + xe_ref[pl.ds(HALO, tl), :] * w[1:2]
         + xe_ref[pl.ds(HALO + 1, tl), :] * w[2:3] + xe_ref[pl.ds(HALO + 2, tl), :] * w[3:4])
    y = _silu(y)
    qscale = jnp.where(s == 0, GDN_DK ** -0.5, 1.0)
    for h in range(GDN_HEADS):
        cs = slice(h * GDN_DK, (h + 1) * GDN_DK)
        yh = y[:, cs]
        nrm = lax.rsqrt(jnp.sum(yh * yh, axis=-1, keepdims=True) + EPS) * qscale
        fac = jnp.where(s == 2, 1.0, nrm)
        o_ref[0, 0, :, cs] = (yh * fac).astype(o_ref.dtype)


def _gdn_prep(p3, conv_w, batch, seq):
    nblk = seq // GPREP_TL
    hb = GPREP_TL // HALO
    last = seq // HALO - 1
    base = P_NQ // 512
    return pl.pallas_call(
        _gdn_prep_kernel,
        grid=(batch, nblk, 3),
        in_specs=[pl.BlockSpec((1, GPREP_TL, 512), lambda b, j, s: (b, j, base + s)),
                  pl.BlockSpec((1, HALO, 512), lambda b, j, s: (b, jnp.maximum(j * hb - 1, 0), base + s)),
                  pl.BlockSpec((1, HALO, 512), lambda b, j, s: (b, jnp.minimum((j + 1) * hb, last), base + s)),
                  pl.BlockSpec((CONV_W, 512), lambda b, j, s: (0, s))],
        out_specs=pl.BlockSpec((1, 1, GPREP_TL, 512), lambda b, j, s: (s, b, j, 0)),
        out_shape=jax.ShapeDtypeStruct((3, batch, seq, 512), BF16),
        scratch_shapes=[pltpu.VMEM((GPREP_TL + 2 * HALO, 512), F32)],
        compiler_params=_cparams(("parallel", "parallel", "arbitrary")),
        name="gdn_prep",
    )(p3, p3, p3, conv_w)


GDN_TL = 512
GDN_CB = 4


def _gdn_kernel(q_ref, k_ref, v_ref, sm_ref, sel_ref, nega_ref, dtb_ref, o_ref, s_ref):
    d = pl.program_id(0)
    nc = GDN_TL // CHUNK

    @pl.when(pl.program_id(2) == 0)
    def _():
        s_ref[...] = jnp.zeros_like(s_ref)

    incl, strict = _tri_masks(d)
    tri = jnp.where(incl, 1.0, 0.0).astype(BF16)
    tri3 = jnp.concatenate([tri, tri, tri], axis=1)
    upper = jnp.where(strict, 1.0, 0.0)
    r_i = lax.broadcasted_iota(jnp.int32, (CHUNK, CHUNK), 0)
    c_i = lax.broadcasted_iota(jnp.int32, (CHUNK, CHUNK), 1)
    eye = jnp.where(r_i == c_i, 1.0, 0.0)
    fwd = d == 0
    H = GDN_HEADS

    def group(gi, carry):
        rows = []
        for ci in range(GDN_CB):
            c = gi * GDN_CB + ci
            ce = jnp.where(fwd, c, nc - 1 - c)
            rows.append(pl.ds(pl.multiple_of(ce * CHUNK, CHUNK), CHUNK))
        sel = [_dot(sm_ref[0, r, :], sel_ref[0]) for r in rows]
        beta = [_sigmoid(x) for x in sel]
        g = [nega_ref[0] * _softplus(x + dtb_ref[0]) for x in sel]
        g2 = [jnp.concatenate([x[:, H + h:H + h + 1] * upper for h in range(H)] + [x], axis=1) for x in g]
        dd = [_dot(tri3, _split3_rows(x)) for x in g2]
        ch = [(ci, h) for ci in range(GDN_CB) for h in range(H)]
        hs = [slice(h * GDN_DK, (h + 1) * GDN_DK) for h in range(H)]
        k_a = [k_ref[0, 0, rows[ci], hs[h]] for ci, h in ch]
        q_a = [q_ref[0, 0, rows[ci], hs[h]] for ci, h in ch]
        v_a = [v_ref[0, 0, rows[ci], hs[h]] for ci, h in ch]
        n_ch = range(len(ch))
        kf = [k_a[i].astype(F32) for i in n_ch]
        gam = [jnp.where(incl, jnp.exp(dd[ci][:, h * CHUNK:(h + 1) * CHUNK]), 0.0) for ci, h in ch]
        bcol = [dd[ci][:, H * CHUNK + H + h:H * CHUNK + H + h + 1] for ci, h in ch]
        blast = [jnp.where(fwd, b[CHUNK - 1:CHUNK], b[0:1]) for b in bcol]
        eb = [jnp.exp(b) for b in bcol]
        bet = [beta[ci][:, h:h + 1] for ci, h in ch]
        kk = [_dot_nt(k_a[i], k_a[i]) for i in n_ch]
        qk = [_dot_nt(q_a[i], k_a[i]) for i in n_ch]
        pw = [jnp.where(strict, -(bet[i] * kk[i] * gam[i]), 0.0) for i in n_ch]
        t = [eye + pw[i] for i in n_ch]
        for _ in range(5):
            pw = [_dot_split(pw[i], pw[i]) for i in n_ch]
            t = [t[i] + _dot_split(t[i], pw[i]) for i in n_ch]
        rhs = [jnp.concatenate([v_a[i].astype(F32) * bet[i], kf[i] * (bet[i] * eb[i])], axis=1).astype(BF16)
               for i in n_ch]
        sol = [_dot(t[i].astype(BF16), rhs[i]) for i in n_ch]
        u = [sol[i][:, :GDN_DV] for i in n_ch]
        w = [sol[i][:, GDN_DV:].astype(BF16) for i in n_ch]
        attn = [(qk[i] * gam[i]).astype(BF16) for i in n_ch]
        qdec = [(q_a[i].astype(F32) * eb[i]).astype(BF16) for i in n_ch]
        kend = [(kf[i] * jnp.exp(blast[i] - bcol[i])).astype(BF16) for i in n_ch]
        dec = [jnp.exp(b) for b in blast]
        st = [s_ref[h] for h in range(H)]
        for ci in range(GDN_CB):
            ix = [ci * H + h for h in range(H)]
            st_b = [st[h].astype(BF16) for h in range(H)]
            vnb = [(u[i] - _dot(w[i], st_b[h])).astype(BF16) for h, i in enumerate(ix)]
            o_new = [(_dot(qdec[i], st_b[h]) + _dot(attn[i], vnb[h])).astype(o_ref.dtype) for h, i in enumerate(ix)]
            st = [dec[i] * st[h] + _dot_tn(kend[i], vnb[h]) for h, i in enumerate(ix)]
            for h in range(H):
                o_ref[0, 0, rows[ci], hs[h]] = o_new[h]
        for h in range(H):
            s_ref[h] = st[h]
        return carry

    lax.fori_loop(0, nc // GDN_CB, group, 0)


def _gdn(qkv, p3, sel_m, nega, dtb, batch, seq):
    nblk = seq // GDN_TL

    def blk(dd, j):
        return j + dd * (nblk - 1 - 2 * j)

    def qspec(idx):
        return pl.BlockSpec((1, 1, GDN_TL, 512), lambda dd, b, j: (idx, b, blk(dd, j), 0))

    return pl.pallas_call(
        _gdn_kernel,
        grid=(2, batch, nblk),
        in_specs=[qspec(0), qspec(1), qspec(2),
                  pl.BlockSpec((1, GDN_TL, LANE), lambda dd, b, j: (b, blk(dd, j), P_SMALL // LANE)),
                  pl.BlockSpec((1, LANE, LANE), lambda dd, b, j: (dd, 0, 0)),
                  pl.BlockSpec((1, 1, LANE), lambda dd, b, j: (dd, 0, 0)),
                  pl.BlockSpec((1, 1, LANE), lambda dd, b, j: (dd, 0, 0))],
        out_specs=pl.BlockSpec((1, 1, GDN_TL, 512), lambda dd, b, j: (dd, b, blk(dd, j), 0)),
        out_shape=jax.ShapeDtypeStruct((2, batch, seq, 512), BF16),
        scratch_shapes=[pltpu.VMEM((GDN_HEADS, GDN_DK, GDN_DV), F32)],
        compiler_params=_cparams(("parallel", "parallel", "arbitrary")),
        name="gdn",
    )(qkv, qkv, qkv, p3, sel_m, nega, dtb)


MERGE_TM = 512


def _head_rms(o, w):
    parts = []
    for h in range(4):
        oh = o[:, h * LANE:(h + 1) * LANE]
        parts.append(oh * lax.rsqrt(jnp.mean(oh * oh, axis=-1, keepdims=True) + EPS))
    return jnp.concatenate(parts, axis=1) * w


def _merge_kernel(gf_ref, gb_ref, gr_ref, od_ref, nf_ref, nb_ref, nz_ref, mg0_ref, mg1_ref, mg2_ref,
                  x_ref, g1_ref, nrm_ref, wb_ref, wo_ref, nw_ref, o_ref, *, diff_scale):
    nrm = nrm_ref[...]
    y_gla = _head_rms(gf_ref[0, 0].astype(F32) + gb_ref[0, 0].astype(F32), nrm[0:1]) * _silu(gr_ref[0].astype(F32))
    y_diff = _head_rms(od_ref[0].astype(F32), nrm[1:2]) * diff_scale
    y_gdn = _head_rms(nf_ref[0, 0].astype(F32) + nb_ref[0, 0].astype(F32), nrm[2:3]) * _silu(nz_ref[0].astype(F32))
    merged = (_sigmoid(mg0_ref[0].astype(F32)) * _dot(y_gla.astype(BF16), wb_ref[0])
              + _sigmoid(mg1_ref[0].astype(F32)) * _dot(y_diff.astype(BF16), wb_ref[1])
              + _sigmoid(mg2_ref[0].astype(F32)) * _dot(y_gdn.astype(BF16), wb_ref[2]))
    y = _dot(merged.astype(BF16), wo_ref[...])
    o_ref[0] = x_ref[0] + g1_ref[0] * _rms_rows(y, nw_ref[...])


def _merge(gla_o, p3, att_o, gdn_o, x3, g1, nrm3, wb, wo, nw, diff_scale, batch, seq):
    tm = MERGE_TM
    nblk = seq // tm
    d = D_MODEL

    def dirspec(idx):
        return pl.BlockSpec((1, 1, tm, 512), lambda b, j: (idx, b, j, 0))

    def pspec(width, col):
        return pl.BlockSpec((1, tm, width), lambda b, j: (b, j, col // width))

    return pl.pallas_call(
        functools.partial(_merge_kernel, diff_scale=diff_scale),
        grid=(batch, nblk),
        in_specs=[dirspec(0), dirspec(1), pspec(512, P_GR),
                  pl.BlockSpec((1, tm, 512), lambda b, j: (b, j, 0)),
                  dirspec(0), dirspec(1), pspec(512, P_NZ),
                  pspec(d, P_MG), pspec(d, P_MG + d), pspec(d, P_MG + 2 * d),
                  pl.BlockSpec((1, tm, d), lambda b, j: (b, j, 0)),
                  pl.BlockSpec((1, 1, d), lambda b, j: (b, 0, 0)),
                  pl.BlockSpec((3, 512), lambda b, j: (0, 0)),
                  pl.BlockSpec((3, 512, d), lambda b, j: (0, 0, 0)),
                  pl.BlockSpec((d, d), lambda b, j: (0, 0)),
                  pl.BlockSpec((1, d), lambda b, j: (0, 0))],
        out_specs=pl.BlockSpec((1, tm, d), lambda b, j: (b, j, 0)),
        out_shape=jax.ShapeDtypeStruct((batch, seq, d), F32),
        compiler_params=_cparams(("parallel", "parallel")),
        name="merge",
    )(gla_o, gla_o, p3, att_o, gdn_o, gdn_o, p3, p3, p3, p3, x3, g1, nrm3, wb, wo, nw)


FFN_TM = 512


def _ffn_kernel(x_ref, nw_ref, sc_ref, sh_ref, w1_ref, w3_ref, w2_ref, g2_ref, nwo_ref, o_ref, h_ref, acc_ref):
    j = pl.program_id(1)

    @pl.when(j == 0)
    def _():
        y = _rms_rows(x_ref[...], nw_ref[...])
        h_ref[...] = (y * (1.0 + sc_ref[0]) + sh_ref[0]).astype(BF16)
        acc_ref[...] = jnp.zeros_like(acc_ref)

    h = h_ref[...]
    a = _dot(h, w1_ref[...])
    b = _dot(h, w3_ref[...])
    acc_ref[...] += _dot((_silu(a) * b).astype(BF16), w2_ref[...])

    @pl.when(j == pl.num_programs(1) - 1)
    def _():
        o_ref[...] = x_ref[...] + g2_ref[0] * _rms_rows(acc_ref[...], nwo_ref[...])


def _ffn(x2, nw, sc, sh, w1, w3, w2, g2, nwo, seq):
    m, d = x2.shape
    f = w1.shape[1]
    tm, tf = FFN_TM, f // 2
    per_b = seq // tm
    vec = pl.BlockSpec((1, d), lambda i, j: (0, 0))
    mod = pl.BlockSpec((1, 1, d), lambda i, j: (i // per_b, 0, 0))
    return pl.pallas_call(
        _ffn_kernel,
        grid=(m // tm, f // tf),
        in_specs=[pl.BlockSpec((tm, d), lambda i, j: (i, 0)), vec, mod, mod,
                  pl.BlockSpec((d, tf), lambda i, j: (0, j)),
                  pl.BlockSpec((d, tf), lambda i, j: (0, j)),
                  pl.BlockSpec((tf, d), lambda i, j: (j, 0)),
                  mod, vec],
        out_specs=pl.BlockSpec((tm, d), lambda i, j: (i, 0)),
        out_shape=jax.ShapeDtypeStruct((m, d), F32),
        scratch_shapes=[pltpu.VMEM((tm, d), BF16), pltpu.VMEM((tm, d), F32)],
        compiler_params=_cparams(("parallel", "arbitrary")),
        name="ffn",
    )(x2, nw, sc, sh, w1, w3, w2, g2, nwo)


ROUTER_TM = 1024


def _router_kernel(x_ref, nw_ref, sc_ref, sh_ref, rw_ref, o_ref, h_ref):
    y = _rms_rows(x_ref[...], nw_ref[...])
    h = y * (1.0 + sc_ref[0]) + sh_ref[0]
    h_ref[...] = h
    logits = jnp.dot(h, rw_ref[...], preferred_element_type=F32, precision=lax.Precision.HIGHEST)
    lane = lax.broadcasted_iota(jnp.int32, logits.shape, 1)
    neg = -jnp.inf
    logits = jnp.where(lane < N_EXPERTS, logits, neg)
    m1 = jnp.max(logits, axis=-1, keepdims=True)
    i1 = jnp.min(jnp.where(logits == m1, lane, LANE), axis=-1, keepdims=True)
    rest = jnp.where(lane == i1, neg, logits)
    m2 = jnp.max(rest, axis=-1, keepdims=True)
    i2 = jnp.min(jnp.where(rest == m2, lane, LANE), axis=-1, keepdims=True)
    e = jnp.exp(m2 - m1)
    w1 = 1.0 / (1.0 + e)
    w2 = e / (1.0 + e)
    o_ref[...] = (jnp.where(lane == 0, i1.astype(F32), 0.0) + jnp.where(lane == 1, i2.astype(F32), 0.0)
                  + jnp.where(lane == 2, w1, 0.0) + jnp.where(lane == 3, w2, 0.0))


def _router(x2, nw, sc, sh, rw_pad, seq):
    m, d = x2.shape
    tm = ROUTER_TM
    per_b = seq // tm
    return pl.pallas_call(
        _router_kernel,
        grid=(m // tm,),
        in_specs=[pl.BlockSpec((tm, d), lambda i: (i, 0)),
                  pl.BlockSpec((1, d), lambda i: (0, 0)),
                  pl.BlockSpec((1, 1, d), lambda i: (i // per_b, 0, 0)),
                  pl.BlockSpec((1, 1, d), lambda i: (i // per_b, 0, 0)),
                  pl.BlockSpec((d, LANE), lambda i: (0, 0))],
        out_specs=[pl.BlockSpec((tm, LANE), lambda i: (i, 0)), pl.BlockSpec((tm, d), lambda i: (i, 0))],
        out_shape=[jax.ShapeDtypeStruct((m, LANE), F32), jax.ShapeDtypeStruct((m, d), F32)],
        compiler_params=_cparams(("parallel",)),
        name="router",
    )(x2, nw, sc, sh, rw_pad)


MOE_TM = 448
MOE_NF = 7


def _moe_plan(rt, m):
    tm = MOE_TM
    n_assign = TOP_K * m
    nt = -(-n_assign // tm) + N_EXPERTS
    e_all = rt[:, :TOP_K].astype(jnp.int32).reshape(-1)
    onehot = (e_all[:, None] == jnp.arange(N_EXPERTS, dtype=jnp.int32)[None, :]).astype(jnp.int32)
    csum = jnp.cumsum(onehot, axis=0)
    rank = jnp.take_along_axis(csum, e_all[:, None], axis=1)[:, 0] - 1
    counts = csum[-1]
    padded = ((counts + tm - 1) // tm) * tm
    ends = jnp.cumsum(padded)
    dest = (ends - padded)[e_all] + rank
    src = jnp.full((nt * tm,), -1, jnp.int32).at[dest].set(jnp.arange(n_assign, dtype=jnp.int32))
    tile_expert = jnp.minimum(jnp.searchsorted(ends, jnp.arange(nt, dtype=jnp.int32) * tm, side="right"),
                              N_EXPERTS - 1).astype(jnp.int32)
    src = src.reshape(nt, tm)
    valid = src >= 0
    gather_tok = jnp.where(valid, src // TOP_K, 0)
    dump_row = m + jnp.arange(tm, dtype=jnp.int32)[None, :]
    tile_par = (jnp.arange(nt, dtype=jnp.int32) % 2)[:, None]
    srow = jnp.concatenate([dump_row, jnp.where(valid, src // TOP_K, dump_row)], axis=0)
    scol = jnp.concatenate([jnp.ones((1, tm), jnp.int32), jnp.where(valid, src % TOP_K, tile_par)], axis=0)
    return (gather_tok.reshape(nt, 1, tm), srow.reshape(nt + 1, 1, tm), scol.reshape(nt + 1, 1, tm), tile_expert)


def _moe_kernel(te_ref, srow_ref, scol_ref, cur_ref, nxt_ref, h_hbm, w1_ref, w3_ref, w2_ref, y_hbm,
                gbuf, obuf, hb_ref, acc_ref, gsem, ssem):
    t = pl.program_id(0)
    j = pl.program_id(1)
    nt = pl.num_programs(0) - 1
    tm = MOE_TM
    d = hb_ref.shape[1]
    per_step = tm // MOE_NF
    slot = t % 2

    def gather_copy(idx_ref, r, sl):
        return pltpu.make_async_copy(h_hbm.at[pl.ds(idx_ref[0, 0, r], 1)], gbuf.at[sl, pl.ds(r, 1)], gsem.at[sl])

    def scatter_copy(r, sl):
        col = pl.multiple_of(scol_ref[0, 0, r] * d, d)
        return pltpu.make_async_copy(obuf.at[sl, pl.ds(r, 1)], y_hbm.at[pl.ds(srow_ref[0, 0, r], 1), pl.ds(col, d)],
                                     ssem.at[sl])

    def wait_gather(sl):
        pltpu.make_async_copy(gbuf.at[sl], gbuf.at[sl], gsem.at[sl]).wait()

    def wait_scatter(sl):
        pltpu.make_async_copy(obuf.at[sl], obuf.at[sl], ssem.at[sl]).wait()

    @pl.when((t == 0) & (j == 0))
    def _():
        def body(r, c):
            gather_copy(cur_ref, r, 0).start()
            return c
        lax.fori_loop(0, tm, body, 0)
        obuf[1] = jnp.zeros((tm, d), F32)

    @pl.when(j == 0)
    def _():
        wait_gather(slot)
        hb_ref[...] = gbuf[slot].astype(BF16)
        acc_ref[...] = jnp.zeros_like(acc_ref)

    base = j * per_step
    for i in range(per_step):
        gather_copy(nxt_ref, base + i, 1 - slot).start()
        scatter_copy(base + i, 1 - slot).start()
    h = hb_ref[...]
    a = _dot(h, w1_ref[0])
    b = _dot(h, w3_ref[0])
    acc_ref[...] += _dot((_silu(a) * b).astype(BF16), w2_ref[0])

    @pl.when(j == MOE_NF - 1)
    def _():
        @pl.when(t > 0)
        def _():
            wait_scatter(slot)

        obuf[slot] = acc_ref[...]

        @pl.when(t == nt)
        def _():
            wait_scatter(1 - slot)
            wait_gather(1 - slot)


def _moe(h, gather_tok, srow, scol, tile_expert, w1, w3, w2):
    m, d = h.shape
    ne, _, f = w1.shape
    tm, tf = MOE_TM, f // MOE_NF
    nt = gather_tok.shape[0]

    def smem_tile(idx):
        return pl.BlockSpec((1, 1, tm), idx, memory_space=pltpu.SMEM)

    def tile(t):
        return jnp.minimum(t, nt - 1)

    grid_spec = pltpu.PrefetchScalarGridSpec(
        num_scalar_prefetch=1,
        grid=(nt + 1, MOE_NF),
        in_specs=[smem_tile(lambda t, j, te: (t, 0, 0)),
                  smem_tile(lambda t, j, te: (t, 0, 0)),
                  smem_tile(lambda t, j, te: (tile(t), 0, 0)),
                  smem_tile(lambda t, j, te: (tile(t + 1), 0, 0)),
                  pl.BlockSpec(memory_space=pl.ANY),
                  pl.BlockSpec((1, d, tf), lambda t, j, te: (te[tile(t)], 0, j)),
                  pl.BlockSpec((1, d, tf), lambda t, j, te: (te[tile(t)], 0, j)),
                  pl.BlockSpec((1, tf, d), lambda t, j, te: (te[tile(t)], j, 0))],
        out_specs=pl.BlockSpec(memory_space=pl.ANY),
        scratch_shapes=[pltpu.VMEM((2, tm, d), F32), pltpu.VMEM((2, tm, d), F32), pltpu.VMEM((tm, d), BF16),
                        pltpu.VMEM((tm, d), F32), pltpu.SemaphoreType.DMA((2,)), pltpu.SemaphoreType.DMA((2,))])
    return pl.pallas_call(
        _moe_kernel,
        grid_spec=grid_spec,
        out_shape=jax.ShapeDtypeStruct((m + tm, TOP_K * d), F32),
        compiler_params=_cparams(("arbitrary", "arbitrary")),
        name="moe",
    )(tile_expert, srow, scol, gather_tok, gather_tok, h, w1, w3, w2)


COMBINE_TM = 512


def _combine_kernel(y_ref, rt_ref, x_ref, g2_ref, nwo_ref, o_ref):
    d = x_ref.shape[1]
    rt = rt_ref[...]
    y = rt[:, 2:3] * y_ref[:, :d] + rt[:, 3:4] * y_ref[:, d:]
    o_ref[...] = x_ref[...] + g2_ref[0] * _rms_rows(y, nwo_ref[...])


def _combine(y2, rt, x2, g2, nwo, seq):
    m, d = x2.shape
    tm = COMBINE_TM
    per_b = seq // tm
    return pl.pallas_call(
        _combine_kernel,
        grid=(m // tm,),
        in_specs=[pl.BlockSpec((tm, TOP_K * d), lambda i: (i, 0)),
                  pl.BlockSpec((tm, LANE), lambda i: (i, 0)),
                  pl.BlockSpec((tm, d), lambda i: (i, 0)),
                  pl.BlockSpec((1, 1, d), lambda i: (i // per_b, 0, 0)),
                  pl.BlockSpec((1, d), lambda i: (0, 0))],
        out_specs=pl.BlockSpec((tm, d), lambda i: (i, 0)),
        out_shape=jax.ShapeDtypeStruct((m, d), F32),
        compiler_params=_cparams(("parallel",)),
        name="moe_combine",
    )(y2, rt, x2, g2, nwo)


def _reorder_w_in(w):
    o = _IN_OFF

    def cols(a, b):
        return w[:, o[a]:o[b]]

    small = jnp.concatenate([cols(4, 6), cols(13, 17)], axis=1)
    pad = jnp.zeros((w.shape[0], P_COLS - P_SMALL - small.shape[1]), w.dtype)
    out = jnp.concatenate([cols(6, 9),
                           cols(2, 4),
                           cols(0, 2),
                           cols(9, 13),
                           cols(17, 18),
                           small, pad], axis=1)
    return out.astype(BF16)


def _gla_gate_params(up, bias):
    hi = up.astype(BF16)
    lo = (up - hi.astype(F32)).astype(BF16)
    out = jnp.zeros((2, 2 * LANE, GLA_HEADS * GLA_DK), BF16)
    for dd, off in ((0, S_GWF), (1, S_GWB)):
        out = out.at[dd, off:off + GLA_RANK].set(hi[dd])
        out = out.at[dd, LANE + off:LANE + off + GLA_RANK].set(lo[dd])
    return out, bias.reshape(2, 1, -1).astype(F32)


def _gdn_gate_params(a_log, dt_bias):
    h = GDN_HEADS
    sel = np.zeros((2, LANE, LANE), np.float32)
    for dd, (boff, aoff) in enumerate(((S_NBF, S_NAF), (S_NBB, S_NAB))):
        for i in range(h):
            sel[dd, boff + i, i] = 1.0
            sel[dd, aoff + i, h + i] = 1.0
    nega = jnp.zeros((2, 1, LANE), F32).at[:, 0, h:2 * h].set(-jnp.exp(a_log.astype(F32)))
    dtb = jnp.zeros((2, 1, LANE), F32).at[:, 0, h:2 * h].set(dt_bias.astype(F32))
    return jnp.asarray(sel, BF16), nega, dtb


def _rope_inv_row():
    inv = ROPE_THETA ** (-jnp.arange(0, ROT_DIM, 2, dtype=F32) / ROT_DIM)
    lane = np.arange(LANE) % DIFF_DH
    row = jnp.where(jnp.asarray(lane < ROT_DIM), inv[jnp.asarray(lane % (ROT_DIM // 2))], 0.0)
    return row.reshape(1, LANE).astype(F32)


def kernel(x, c, positions, adaln_w, adaln_b, norm_w, w_in, gla_gate_up, gla_gate_bias, gla_norm, diff_lambda,
           diff_norm, gdn_conv, gdn_A_log, gdn_dt_bias, gdn_norm, w_branch, w_out, ffn_w1, ffn_w3, ffn_w2,
           router_w, moe_w1, moe_w3, moe_w2):
    batch, seq, d = x.shape
    depth = adaln_w.shape[0]
    m = batch * seq

    c_pad = jnp.zeros((8, d), F32).at[:batch].set(c)
    mod = _adaln(c_pad, adaln_w, adaln_b)[:, :batch]
    pos3 = positions.reshape(batch, seq, 1)
    inv_row = _rope_inv_row()

    for layer in range(depth):
        sh1, sc1, g1, sh2, sc2, g2 = [t.reshape(batch, 1, d) for t in jnp.split(mod[layer], 6, axis=-1)]
        nw = norm_w[layer].astype(F32)

        p = _inproj(x.reshape(m, d), nw[0:1], sc1, sh1, _reorder_w_in(w_in[layer]), seq)
        p3 = p.reshape(batch, seq, P_COLS)

        up2, gbias = _gla_gate_params(gla_gate_up[layer], gla_gate_bias[layer])
        gla_o = _gla(p3, up2, gbias, batch, seq)

        qk = _rope(p3, pos3, inv_row, batch, seq)
        lam_init = 0.8 - 0.6 * math.exp(-0.3 * layer)
        att_o = _attn(qk, p3, diff_lambda[layer].astype(F32), lam_init, batch, seq)

        qkv = _gdn_prep(p3, gdn_conv[layer].astype(F32), batch, seq)
        sel_m, nega, dtb = _gdn_gate_params(gdn_A_log[layer], gdn_dt_bias[layer])
        gdn_o = _gdn(qkv, p3, sel_m, nega, dtb, batch, seq)

        nrm3 = jnp.stack([jnp.tile(gla_norm[layer], 4), jnp.tile(diff_norm[layer], 4),
                          jnp.tile(gdn_norm[layer], 4)]).astype(F32)
        x = _merge(gla_o, p3, att_o, gdn_o, x, g1, nrm3, w_branch[layer].astype(BF16),
                   w_out[layer].astype(BF16), nw[1:2], 1.0 - lam_init, batch, seq)

        x2 = x.reshape(m, d)
        if layer % 2 == 0:
            li = layer // 2
            x2 = _ffn(x2, nw[2:3], sc2, sh2, ffn_w1[li].astype(BF16), ffn_w3[li].astype(BF16),
                      ffn_w2[li].astype(BF16), g2, nw[3:4], seq)
        else:
            li = layer // 2
            rw_pad = jnp.zeros((d, LANE), F32).at[:, :N_EXPERTS].set(router_w[li])
            rt, h2 = _router(x2, nw[2:3], sc2, sh2, rw_pad, seq)
            y = _moe(h2, *_moe_plan(rt, m), moe_w1[li].astype(BF16), moe_w3[li].astype(BF16),
                     moe_w2[li].astype(BF16))
            x2 = _combine(y, rt, x2, g2, nw[3:4], seq)
        x = x2.reshape(batch, seq, d)
    return x
```

```python
import functools
import math

import jax
import jax.numpy as jnp
import numpy as np
from jax import lax
from jax.experimental import pallas as pl
from jax.experimental.pallas import tpu as pltpu

F32 = jnp.float32
BF16 = jnp.bfloat16

D_MODEL = 1024
GLA_HEADS, GLA_DK, GLA_DV, GLA_RANK, GLA_TAU = 4, 64, 128, 16, 16.0
DIFF_HEADS, DIFF_DH = 4, 64
ROT_DIM = DIFF_DH // 4
ROPE_THETA = 500000.0
GDN_HEADS, GDN_DK, GDN_DV, CONV_W = 4, 128, 128, 4
CHUNK = 64
N_BRANCH = 3
N_EXPERTS, TOP_K = 8, 2
EPS = 1e-6
BRANCH_W = 512

_IN_SIZES = (256, 256, 512, 512, 16, 16, 512, 512, 512, 512, 512, 512, 512, 4, 4, 4, 4, 3 * D_MODEL)
_IN_OFF = tuple(int(v) for v in np.concatenate([[0], np.cumsum(_IN_SIZES)]))

P_DQ, P_DK, P_DV = 0, 512, 1024
P_GV, P_GR = 1536, 2048
P_GQ, P_GK = 2560, 2816
P_NQ, P_NK, P_NV, P_NZ = 3072, 3584, 4096, 4608
P_MG = 5120
P_SMALL = 8192
P_COLS = 8448
S_GWF, S_GWB, S_NBF, S_NBB, S_NAF, S_NAB = 0, 16, 32, 36, 40, 44

VMEM_LIMIT = 56 * 1024 * 1024
LANE = 128


def _cparams(sem):
    return pltpu.CompilerParams(dimension_semantics=sem, vmem_limit_bytes=VMEM_LIMIT)


def _dot(a, b):
    return jnp.dot(a, b, preferred_element_type=F32)


def _dot_nt(a, b):
    return lax.dot_general(a, b, (((1,), (1,)), ((), ())), preferred_element_type=F32)


def _dot_tn(a, b):
    return lax.dot_general(a, b, (((0,), (0,)), ((), ())), preferred_element_type=F32)


def _split3_rows(x):
    hi = x.astype(BF16)
    r1 = x - hi.astype(F32)
    mid = r1.astype(BF16)
    lo = (r1 - mid.astype(F32)).astype(BF16)
    return jnp.concatenate([hi, mid, lo], axis=0)


def _dot_split(a, b):
    a_hi = a.astype(BF16)
    a_lo = (a - a_hi.astype(F32)).astype(BF16)
    b_hi = b.astype(BF16)
    b_lo = (b - b_hi.astype(F32)).astype(BF16)
    return _dot(jnp.concatenate([a_hi, a_lo, a_hi], axis=1), jnp.concatenate([b_hi, b_hi, b_lo], axis=0))


def _sigmoid(x):
    return 1.0 / (1.0 + jnp.exp(-x))


def _silu(x):
    return x * _sigmoid(x)


def _softplus(x):
    return jnp.maximum(x, 0.0) + jnp.log1p(jnp.exp(-jnp.abs(x)))


def _rms_rows(x, w):
    return x * lax.rsqrt(jnp.mean(x * x, axis=-1, keepdims=True) + EPS) * w


def _tri_masks(d):
    r = lax.broadcasted_iota(jnp.int32, (CHUNK, CHUNK), 0)
    c = lax.broadcasted_iota(jnp.int32, (CHUNK, CHUNK), 1)
    lag = (r - c) * (1 - 2 * d)
    return lag >= 0, lag > 0


def _adaln_kernel(c_ref, w_ref, b_ref, o_ref):
    c = c_ref[...]
    o_ref[0] = jnp.dot(_silu(c), w_ref[0], preferred_element_type=F32,
                       precision=lax.Precision.HIGHEST) + b_ref[0]


def _adaln(c_pad, adaln_w, adaln_b):
    depth, d, n = adaln_w.shape
    tn = 1536
    return pl.pallas_call(
        _adaln_kernel,
        grid=(depth, n // tn),
        in_specs=[pl.BlockSpec((8, d), lambda l, j: (0, 0)),
                  pl.BlockSpec((1, d, tn), lambda l, j: (l, 0, j)),
                  pl.BlockSpec((1, 1, tn), lambda l, j: (l, 0, j))],
        out_specs=pl.BlockSpec((1, 8, tn), lambda l, j: (l, 0, j)),
        out_shape=jax.ShapeDtypeStruct((depth, 8, n), F32),
        compiler_params=_cparams(("parallel", "parallel")),
        name="adaln",
    )(c_pad, adaln_w, adaln_b.reshape(depth, 1, n))


def _inproj_kernel(x_ref, nw_ref, sc_ref, sh_ref, w_ref, o_ref):
    h = (_rms_rows(x_ref[...], nw_ref[...]) * (1.0 + sc_ref[0]) + sh_ref[0]).astype(BF16)
    o_ref[...] = _dot(h, w_ref[...]).astype(o_ref.dtype)


def _resident(shape):
    return pl.BlockSpec(shape, lambda *_: (0,) * len(shape), pipeline_mode=pl.Buffered(1))


def _inproj(x2, nw, sc, sh, w, seq):
    m, d = x2.shape
    n = w.shape[1]
    tm = 512
    per_b = seq // tm
    return pl.pallas_call(
        _inproj_kernel,
        grid=(m // tm,),
        in_specs=[pl.BlockSpec((tm, d), lambda i: (i, 0)),
                  pl.BlockSpec((1, d), lambda i: (0, 0)),
                  pl.BlockSpec((1, 1, d), lambda i: (i // per_b, 0, 0)),
                  pl.BlockSpec((1, 1, d), lambda i: (i // per_b, 0, 0)),
                  _resident((d, n))],
        out_specs=pl.BlockSpec((tm, n), lambda i: (i, 0)),
        out_shape=jax.ShapeDtypeStruct((m, n), BF16),
        compiler_params=_cparams(("parallel",)),
        name="inproj",
    )(x2, nw, sc, sh, w)


GLA_TL = 512
GLA_CB = 8


def _gla_kernel(q_ref, k_ref, v_ref, sm_ref, up_ref, bias_ref, o_ref, st_ref):
    d = pl.program_id(0)
    nc = GLA_TL // CHUNK

    @pl.when(pl.program_id(2) == 0)
    def _():
        st_ref[...] = jnp.zeros_like(st_ref)

    incl, _ = _tri_masks(d)
    tri = jnp.where(incl, 1.0, 0.0).astype(BF16)
    tri3 = jnp.concatenate([tri, tri, tri], axis=1)
    fwd = d == 0
    scale = GLA_DK ** -0.5

    H = GLA_HEADS
    ks = [slice(h * GLA_DK, (h + 1) * GLA_DK) for h in range(H)]
    vs = [slice(h * GLA_DV, (h + 1) * GLA_DV) for h in range(H)]

    def group(gi, carry):
        rows = []
        for ci in range(GLA_CB):
            c = gi * GLA_CB + ci
            ce = jnp.where(fwd, c, nc - 1 - c)
            rows.append(pl.ds(pl.multiple_of(ce * CHUNK, CHUNK), CHUNK))
        cr = range(GLA_CB)
        sm = [sm_ref[0, r, :] for r in rows]
        z = [_dot(jnp.concatenate([x, x], axis=1), up_ref[0]) + bias_ref[0] for x in sm]
        g = [(jnp.minimum(x, 0.0) - jnp.log1p(jnp.exp(-jnp.abs(x)))) * (1.0 / GLA_TAU) for x in z]
        b = [_dot(tri3, _split3_rows(x)) for x in g]
        b_last = [jnp.where(fwd, x[CHUNK - 1:CHUNK], x[0:1]) for x in b]
        b_mid = [jnp.where(fwd, x[CHUNK // 2 - 1:CHUNK // 2], x[CHUNK // 2:CHUNK // 2 + 1]) for x in b]
        q = [q_ref[0, r, :].astype(F32) * scale for r in rows]
        k = [k_ref[0, r, :].astype(F32) for r in rows]
        qe = [(q[i] * jnp.exp(b[i] - b_mid[i])).astype(BF16) for i in cr]
        ke = [(k[i] * jnp.exp(b_mid[i] - b[i])).astype(BF16) for i in cr]
        kend = [(k[i] * jnp.exp(b_last[i] - b[i])).astype(BF16) for i in cr]
        qd = [(q[i] * jnp.exp(b[i])).astype(BF16) for i in cr]
        dec = [jnp.exp(x) for x in b_last]
        ch = [(ci, h) for ci in cr for h in range(H)]
        v_a = [v_ref[0, rows[ci], vs[h]] for ci, h in ch]
        a = [jnp.where(incl, _dot_nt(qe[ci][:, ks[h]], ke[ci][:, ks[h]]), 0.0).astype(BF16) for ci, h in ch]
        o_in = [_dot(a[i], v_a[i]) for i in range(len(ch))]
        kv = [_dot_tn(v_a[i], kend[ci][:, ks[h]]) for i, (ci, h) in enumerate(ch)]
        st = st_ref[...]
        for ci in cr:
            st_b = st.astype(BF16)
            for h in range(H):
                o = o_in[ci * H + h] + _dot_nt(qd[ci][:, ks[h]], st_b[:, ks[h]])
                o_ref[0, 0, rows[ci], vs[h]] = o.astype(o_ref.dtype)
            st = st * dec[ci] + jnp.concatenate([kv[ci * H + h] for h in range(H)], axis=1)
        st_ref[...] = st
        return carry

    lax.fori_loop(0, nc // GLA_CB, group, 0)


def _gla(p3, up2, bias, batch, seq):
    nblk = seq // GLA_TL

    def blk(dd, j):
        return j + dd * (nblk - 1 - 2 * j)

    return pl.pallas_call(
        _gla_kernel,
        grid=(2, batch, nblk),
        in_specs=[pl.BlockSpec((1, GLA_TL, 256), lambda dd, b, j: (b, blk(dd, j), P_GQ // 256)),
                  pl.BlockSpec((1, GLA_TL, 256), lambda dd, b, j: (b, blk(dd, j), P_GK // 256)),
                  pl.BlockSpec((1, GLA_TL, 512), lambda dd, b, j: (b, blk(dd, j), P_GV // 512)),
                  pl.BlockSpec((1, GLA_TL, LANE), lambda dd, b, j: (b, blk(dd, j), P_SMALL // LANE)),
                  pl.BlockSpec((1, 2 * LANE, 256), lambda dd, b, j: (dd, 0, 0)),
                  pl.BlockSpec((1, 1, 256), lambda dd, b, j: (dd, 0, 0))],
        out_specs=pl.BlockSpec((1, 1, GLA_TL, 512), lambda dd, b, j: (dd, b, blk(dd, j), 0)),
        out_shape=jax.ShapeDtypeStruct((2, batch, seq, 512), BF16),
        scratch_shapes=[pltpu.VMEM((GLA_DV, GLA_HEADS * GLA_DK), F32)],
        compiler_params=_cparams(("parallel", "parallel", "arbitrary")),
        name="gla",
    )(p3, p3, p3, p3, up2, bias)


ROPE_TL = 1024


def _rope_kernel(q_ref, k_ref, pos_ref, inv_ref, o_ref):
    ang = pos_ref[0].astype(F32) * inv_ref[...]
    cosf = jnp.cos(ang)
    sinf = jnp.sin(ang)
    lane = lax.broadcasted_iota(jnp.int32, (1, LANE), 1) % DIFF_DH
    half = ROT_DIM // 2
    cmul = jnp.where(lane < ROT_DIM, cosf, 1.0)
    s_up = jnp.where(lane < half, -sinf, 0.0)
    s_dn = jnp.where((lane >= half) & (lane < ROT_DIM), sinf, 0.0)
    for s, (x_ref, scale) in enumerate(((q_ref, DIFF_DH ** -0.5 * math.log2(math.e)), (k_ref, 1.0))):
        for gidx in range(4):
            cs = slice(gidx * LANE, (gidx + 1) * LANE)
            x = x_ref[0, :, cs].astype(F32)
            y = x * cmul + pltpu.roll(x, LANE - half, 1) * s_up + pltpu.roll(x, half, 1) * s_dn
            o_ref[s, 0, :, cs] = (y * scale).astype(o_ref.dtype)


def _rope(p3, pos3, inv_row, batch, seq):
    nblk = seq // ROPE_TL
    return pl.pallas_call(
        _rope_kernel,
        grid=(batch, nblk),
        in_specs=[pl.BlockSpec((1, ROPE_TL, 512), lambda b, j: (b, j, P_DQ // 512)),
                  pl.BlockSpec((1, ROPE_TL, 512), lambda b, j: (b, j, P_DK // 512)),
                  pl.BlockSpec((1, ROPE_TL, 1), lambda b, j: (b, j, 0)),
                  pl.BlockSpec((1, LANE), lambda b, j: (0, 0))],
        out_specs=pl.BlockSpec((2, 1, ROPE_TL, 512), lambda b, j: (0, b, j, 0)),
        out_shape=jax.ShapeDtypeStruct((2, batch, seq, 512), BF16),
        compiler_params=_cparams(("parallel", "parallel")),
        name="rope",
    )(p3, p3, pos3, inv_row)


ATT_TQ = 512
ATT_TK = 2048


def _attn_kernel(q_ref, k_ref, v_ref, lam_ref, o_ref, sa_ref, sb_ref, *, lam_init):
    q = q_ref[0, 0]
    lane = lax.broadcasted_iota(jnp.int32, q.shape, 1)
    zero = jnp.zeros_like(q)
    q1 = jnp.where(lane < DIFF_DH, q, zero)
    q2 = jnp.where(lane >= DIFF_DH, q, zero)
    tq = q.shape[0]
    tk = sa_ref.shape[2]
    nk = k_ref.shape[2] // tk
    ones_col = jnp.where(lax.broadcasted_iota(jnp.int32, (tk, LANE), 1) == 0, 1.0, 0.0).astype(BF16)

    def soft(s, v1, m, l, acc):
        m_new = jnp.maximum(m, jnp.max(s, axis=-1, keepdims=True))
        p = jnp.exp2((s - m_new).astype(BF16))
        alpha = jnp.exp2(m - m_new)
        pv = _dot(p, v1)
        return m_new, alpha * l + pv[:, LANE:LANE + 1], alpha * acc + pv[:, :LANE]

    def k_tile(t):
        return k_ref[0, 0, pl.ds(t * tk, tk), :]

    def v_tile(t):
        return jnp.concatenate([v_ref[0, pl.ds(t * tk, tk), :], ones_col], axis=1)

    def scores(t, dst):
        k = k_tile(t)
        dst[0] = _dot_nt(q1, k)
        dst[1] = _dot_nt(q2, k)

    def consume(t, src, carry):
        m1, l1, a1, m2, l2, a2 = carry
        v1 = v_tile(t)
        m1, l1, a1 = soft(src[0], v1, m1, l1, a1)
        m2, l2, a2 = soft(src[1], v1, m2, l2, a2)
        return m1, l1, a1, m2, l2, a2

    neg = jnp.full((tq, 1), -jnp.inf, F32)
    zl = jnp.zeros((tq, 1), F32)
    za = jnp.zeros((tq, LANE), F32)
    carry = (neg, zl, za, neg, zl, za)
    bufs = (sa_ref, sb_ref)
    scores(0, bufs[0])
    for t in range(nk):
        if t + 1 < nk:
            scores(t + 1, bufs[(t + 1) % 2])
        carry = consume(t, bufs[t % 2], carry)
    m1, l1, a1, m2, l2, a2 = carry
    lp = lam_ref[...]
    lam = (jnp.exp(jnp.sum(lp[0:1] * lp[1:2], axis=-1, keepdims=True))
           - jnp.exp(jnp.sum(lp[2:3] * lp[3:4], axis=-1, keepdims=True)) + lam_init)
    o_ref[0] = (a1 / l1 - lam * (a2 / l2)).astype(o_ref.dtype)


def _attn(qk, p3, lam_p, lam_init, batch, seq):
    nq = seq // ATT_TQ
    tk = min(ATT_TK, seq // 2)
    return pl.pallas_call(
        functools.partial(_attn_kernel, lam_init=lam_init),
        grid=(batch, DIFF_HEADS, nq),
        in_specs=[pl.BlockSpec((1, 1, ATT_TQ, LANE), lambda b, h, i: (0, b, i, h)),
                  pl.BlockSpec((1, 1, seq, LANE), lambda b, h, i: (1, b, 0, h)),
                  pl.BlockSpec((1, seq, LANE), lambda b, h, i: (b, 0, P_DV // LANE + h)),
                  pl.BlockSpec((4, DIFF_DH), lambda b, h, i: (0, 0))],
        out_specs=pl.BlockSpec((1, ATT_TQ, LANE), lambda b, h, i: (b, i, h)),
        out_shape=jax.ShapeDtypeStruct((batch, seq, 512), BF16),
        scratch_shapes=[pltpu.VMEM((2, ATT_TQ, tk), F32), pltpu.VMEM((2, ATT_TQ, tk), F32)],
        compiler_params=_cparams(("parallel", "parallel", "arbitrary")),
        name="diffattn",
    )(qk, qk, p3, lam_p)


GPREP_TL = 1024
HALO = 16


def _gdn_prep_kernel(x_ref, prev_ref, next_ref, w_ref, o_ref, xe_ref):
    j = pl.program_id(1)
    s = pl.program_id(2)
    nblk = pl.num_programs(1)
    tl = GPREP_TL
    prev = prev_ref[0].astype(F32)
    nxt = next_ref[0].astype(F32)
    xe_ref[0:HALO, :] = jnp.where(j == 0, 0.0, prev)
    xe_ref[HALO:HALO + tl, :] = x_ref[0].astype(F32)
    xe_ref[HALO + tl:2 * HALO + tl, :] = jnp.where(j == nblk - 1, 0.0, nxt)
    w = w_ref[...]
    y = (xe_ref[pl.ds(HALO - 1, tl), :] * w[0:1] + xe_ref[pl.ds(HALO, tl), :] * w[1:2]
         + xe_ref[pl.ds(HALO + 1, tl), :] * w[2:3] + xe_ref[pl.ds(HALO + 2, tl), :] * w[3:4])
    y = _silu(y)
    qscale = jnp.where(s == 0, GDN_DK ** -0.5, 1.0)
    for h in range(GDN_HEADS):
        cs = slice(h * GDN_DK, (h + 1) * GDN_DK)
        yh = y[:, cs]
        nrm = lax.rsqrt(jnp.sum(yh * yh, axis=-1, keepdims=True) + EPS) * qscale
        fac = jnp.where(s == 2, 1.0, nrm)
        o_ref[0, 0, :, cs] = (yh * fac).astype(o_ref.dtype)


def _gdn_prep(p3, conv_w, batch, seq):
    nblk = seq // GPREP_TL
    hb = GPREP_TL // HALO
    last = seq // HALO - 1
    base = P_NQ // 512
    return pl.pallas_call(
        _gdn_prep_kernel,
        grid=(batch, nblk, 3),
        in_specs=[pl.BlockSpec((1, GPREP_TL, 512), lambda b, j, s: (b, j, base + s)),
                  pl.BlockSpec((1, HALO, 512), lambda b, j, s: (b, jnp.maximum(j * hb - 1, 0), base + s)),
                  pl.BlockSpec((1, HALO, 512), lambda b, j, s: (b, jnp.minimum((j + 1) * hb, last), base + s)),
                  pl.BlockSpec((CONV_W, 512), lambda b, j, s: (0, s))],
        out_specs=pl.BlockSpec((1, 1, GPREP_TL, 512), lambda b, j, s: (s, b, j, 0)),
        out_shape=jax.ShapeDtypeStruct((3, batch, seq, 512), BF16),
        scratch_shapes=[pltpu.VMEM((GPREP_TL + 2 * HALO, 512), F32)],
        compiler_params=_cparams(("parallel", "parallel", "arbitrary")),
        name="gdn_prep",
    )(p3, p3, p3, conv_w)


GDN_TL = 512
GDN_CB = 4


def _gdn_kernel(q_ref, k_ref, v_ref, sm_ref, sel_ref, nega_ref, dtb_ref, o_ref, s_ref):
    d = pl.program_id(0)
    nc = GDN_TL // CHUNK

    @pl.when(pl.program_id(2) == 0)
    def _():
        s_ref[...] = jnp.zeros_like(s_ref)

    incl, strict = _tri_masks(d)
    tri = jnp.where(incl, 1.0, 0.0).astype(BF16)
    tri3 = jnp.concatenate([tri, tri, tri], axis=1)
    upper = jnp.where(strict, 1.0, 0.0)
    r_i = lax.broadcasted_iota(jnp.int32, (CHUNK, CHUNK), 0)
    c_i = lax.broadcasted_iota(jnp.int32, (CHUNK, CHUNK), 1)
    eye = jnp.where(r_i == c_i, 1.0, 0.0)
    fwd = d == 0
    H = GDN_HEADS

    def group(gi, carry):
        rows = []
        for ci in range(GDN_CB):
            c = gi * GDN_CB + ci
            ce = jnp.where(fwd, c, nc - 1 - c)
            rows.append(pl.ds(pl.multiple_of(ce * CHUNK, CHUNK), CHUNK))
        sel = [_dot(sm_ref[0, r, :], sel_ref[0]) for r in rows]
        beta = [_sigmoid(x) for x in sel]
        g = [nega_ref[0] * _softplus(x + dtb_ref[0]) for x in sel]
        g2 = [jnp.concatenate([x[:, H + h:H + h + 1] * upper for h in range(H)] + [x], axis=1) for x in g]
        dd = [_dot(tri3, _split3_rows(x)) for x in g2]
        ch = [(ci, h) for ci in range(GDN_CB) for h in range(H)]
        hs = [slice(h * GDN_DK, (h + 1) * GDN_DK) for h in range(H)]
        k_a = [k_ref[0, 0, rows[ci], hs[h]] for ci, h in ch]
        q_a = [q_ref[0, 0, rows[ci], hs[h]] for ci, h in ch]
        v_a = [v_ref[0, 0, rows[ci], hs[h]] for ci, h in ch]
        n_ch = range(len(ch))
        kf = [k_a[i].astype(F32) for i in n_ch]
        gam = [jnp.where(incl, jnp.exp(dd[ci][:, h * CHUNK:(h + 1) * CHUNK]), 0.0) for ci, h in ch]
        bcol = [dd[ci][:, H * CHUNK + H + h:H * CHUNK + H + h + 1] for ci, h in ch]
        blast = [jnp.where(fwd, b[CHUNK - 1:CHUNK], b[0:1]) for b in bcol]
        eb = [jnp.exp(b) for b in bcol]
        bet = [beta[ci][:, h:h + 1] for ci, h in ch]
        kk = [_dot_nt(k_a[i], k_a[i]) for i in n_ch]
        qk = [_dot_nt(q_a[i], k_a[i]) for i in n_ch]
        pw = [jnp.where(strict, -(bet[i] * kk[i] * gam[i]), 0.0) for i in n_ch]
        t = [eye + pw[i] for i in n_ch]
        for _ in range(5):
            pw = [_dot_split(pw[i], pw[i]) for i in n_ch]
            t = [t[i] + _dot_split(t[i], pw[i]) for i in n_ch]
        rhs = [jnp.concatenate([v_a[i].astype(F32) * bet[i], kf[i] * (bet[i] * eb[i])], axis=1).astype(BF16)
               for i in n_ch]
        sol = [_dot(t[i].astype(BF16), rhs[i]) for i in n_ch]
        u = [sol[i][:, :GDN_DV] for i in n_ch]
        w = [sol[i][:, GDN_DV:].astype(BF16) for i in n_ch]
        attn = [(qk[i] * gam[i]).astype(BF16) for i in n_ch]
        qdec = [(q_a[i].astype(F32) * eb[i]).astype(BF16) for i in n_ch]
        kend = [(kf[i] * jnp.exp(blast[i] - bcol[i])).astype(BF16) for i in n_ch]
        dec = [jnp.exp(b) for b in blast]
        st = [s_ref[h] for h in range(H)]
        for ci in range(GDN_CB):
            ix = [ci * H + h for h in range(H)]
            st_b = [st[h].astype(BF16) for h in range(H)]
            vnb = [(u[i] - _dot(w[i], st_b[h])).astype(BF16) for h, i in enumerate(ix)]
            o_new = [(_dot(qdec[i], st_b[h]) + _dot(attn[i], vnb[h])).astype(o_ref.dtype) for h, i in enumerate(ix)]
            st = [dec[i] * st[h] + _dot_tn(kend[i], vnb[h]) for h, i in enumerate(ix)]
            for h in range(H):
                o_ref[0, 0, rows[ci], hs[h]] = o_new[h]
        for h in range(H):
            s_ref[h] = st[h]
        return carry

    lax.fori_loop(0, nc // GDN_CB, group, 0)


def _gdn(qkv, p3, sel_m, nega, dtb, batch, seq):
    nblk = seq // GDN_TL

    def blk(dd, j):
        return j + dd * (nblk - 1 - 2 * j)

    def qspec(idx):
        return pl.BlockSpec((1, 1, GDN_TL, 512), lambda dd, b, j: (idx, b, blk(dd, j), 0))

    return pl.pallas_call(
        _gdn_kernel,
        grid=(2, batch, nblk),
        in_specs=[qspec(0), qspec(1), qspec(2),
                  pl.BlockSpec((1, GDN_TL, LANE), lambda dd, b, j: (b, blk(dd, j), P_SMALL // LANE)),
                  pl.BlockSpec((1, LANE, LANE), lambda dd, b, j: (dd, 0, 0)),
                  pl.BlockSpec((1, 1, LANE), lambda dd, b, j: (dd, 0, 0)),
                  pl.BlockSpec((1, 1, LANE), lambda dd, b, j: (dd, 0, 0))],
        out_specs=pl.BlockSpec((1, 1, GDN_TL, 512), lambda dd, b, j: (dd, b, blk(dd, j), 0)),
        out_shape=jax.ShapeDtypeStruct((2, batch, seq, 512), BF16),
        scratch_shapes=[pltpu.VMEM((GDN_HEADS, GDN_DK, GDN_DV), F32)],
        compiler_params=_cparams(("parallel", "parallel", "arbitrary")),
        name="gdn",
    )(qkv, qkv, qkv, p3, sel_m, nega, dtb)


MERGE_TM = 512


def _head_rms(o, w):
    parts = []
    for h in range(4):
        oh = o[:, h * LANE:(h + 1) * LANE]
        parts.append(oh * lax.rsqrt(jnp.mean(oh * oh, axis=-1, keepdims=True) + EPS))
    return jnp.concatenate(parts, axis=1) * w


def _merge_kernel(gf_ref, gb_ref, gr_ref, od_ref, nf_ref, nb_ref, nz_ref, mg0_ref, mg1_ref, mg2_ref,
                  x_ref, g1_ref, nrm_ref, wb_ref, wo_ref, nw_ref, o_ref, *, diff_scale):
    nrm = nrm_ref[...]
    y_gla = _head_rms(gf_ref[0, 0].astype(F32) + gb_ref[0, 0].astype(F32), nrm[0:1]) * _silu(gr_ref[0].astype(F32))
    y_diff = _head_rms(od_ref[0].astype(F32), nrm[1:2]) * diff_scale
    y_gdn = _head_rms(nf_ref[0, 0].astype(F32) + nb_ref[0, 0].astype(F32), nrm[2:3]) * _silu(nz_ref[0].astype(F32))
    merged = (_sigmoid(mg0_ref[0].astype(F32)) * _dot(y_gla.astype(BF16), wb_ref[0])
              + _sigmoid(mg1_ref[0].astype(F32)) * _dot(y_diff.astype(BF16), wb_ref[1])
              + _sigmoid(mg2_ref[0].astype(F32)) * _dot(y_gdn.astype(BF16), wb_ref[2]))
    y = _dot(merged.astype(BF16), wo_ref[...])
    o_ref[0] = x_ref[0] + g1_ref[0] * _rms_rows(y, nw_ref[...])


def _merge(gla_o, p3, att_o, gdn_o, x3, g1, nrm3, wb, wo, nw, diff_scale, batch, seq):
    tm = MERGE_TM
    nblk = seq // tm
    d = D_MODEL

    def dirspec(idx):
        return pl.BlockSpec((1, 1, tm, 512), lambda b, j: (idx, b, j, 0))

    def pspec(width, col):
        return pl.BlockSpec((1, tm, width), lambda b, j: (b, j, col // width))

    return pl.pallas_call(
        functools.partial(_merge_kernel, diff_scale=diff_scale),
        grid=(batch, nblk),
        in_specs=[dirspec(0), dirspec(1), pspec(512, P_GR),
                  pl.BlockSpec((1, tm, 512), lambda b, j: (b, j, 0)),
                  dirspec(0), dirspec(1), pspec(512, P_NZ),
                  pspec(d, P_MG), pspec(d, P_MG + d), pspec(d, P_MG + 2 * d),
                  pl.BlockSpec((1, tm, d), lambda b, j: (b, j, 0)),
                  pl.BlockSpec((1, 1, d), lambda b, j: (b, 0, 0)),
                  pl.BlockSpec((3, 512), lambda b, j: (0, 0)),
                  pl.BlockSpec((3, 512, d), lambda b, j: (0, 0, 0)),
                  pl.BlockSpec((d, d), lambda b, j: (0, 0)),
                  pl.BlockSpec((1, d), lambda b, j: (0, 0))],
        out_specs=pl.BlockSpec((1, tm, d), lambda b, j: (b, j, 0)),
        out_shape=jax.ShapeDtypeStruct((batch, seq, d), F32),
        compiler_params=_cparams(("parallel", "parallel")),
        name="merge",
    )(gla_o, gla_o, p3, att_o, gdn_o, gdn_o, p3, p3, p3, p3, x3, g1, nrm3, wb, wo, nw)


FFN_TM = 512


def _ffn_kernel(x_ref, nw_ref, sc_ref, sh_ref, w1_ref, w3_ref, w2_ref, g2_ref, nwo_ref, o_ref):
    x = x_ref[...]
    h = (_rms_rows(x, nw_ref[...]) * (1.0 + sc_ref[0]) + sh_ref[0]).astype(BF16)
    a = _dot(h, w1_ref[...])
    b = _dot(h, w3_ref[...])
    y = _dot((_silu(a) * b).astype(BF16), w2_ref[...])
    o_ref[...] = x + g2_ref[0] * _rms_rows(y, nwo_ref[...])


def _ffn(x2, nw, sc, sh, w1, w3, w2, g2, nwo, seq):
    m, d = x2.shape
    f = w1.shape[1]
    tm = FFN_TM
    per_b = seq // tm
    vec = pl.BlockSpec((1, d), lambda i: (0, 0))
    mod = pl.BlockSpec((1, 1, d), lambda i: (i // per_b, 0, 0))
    return pl.pallas_call(
        _ffn_kernel,
        grid=(m // tm,),
        in_specs=[pl.BlockSpec((tm, d), lambda i: (i, 0)), vec, mod, mod,
                  _resident((d, f)), _resident((d, f)), _resident((f, d)), mod, vec],
        out_specs=pl.BlockSpec((tm, d), lambda i: (i, 0)),
        out_shape=jax.ShapeDtypeStruct((m, d), F32),
        compiler_params=_cparams(("parallel",)),
        name="ffn",
    )(x2, nw, sc, sh, w1, w3, w2, g2, nwo)


ROUTER_TM = 1024


def _router_kernel(x_ref, nw_ref, sc_ref, sh_ref, rw_ref, o_ref, h_ref):
    y = _rms_rows(x_ref[...], nw_ref[...])
    h = y * (1.0 + sc_ref[0]) + sh_ref[0]
    h_ref[...] = h
    logits = jnp.dot(h, rw_ref[...], preferred_element_type=F32, precision=lax.Precision.HIGHEST)
    lane = lax.broadcasted_iota(jnp.int32, logits.shape, 1)
    neg = -jnp.inf
    logits = jnp.where(lane < N_EXPERTS, logits, neg)
    m1 = jnp.max(logits, axis=-1, keepdims=True)
    i1 = jnp.min(jnp.where(logits == m1, lane, LANE), axis=-1, keepdims=True)
    rest = jnp.where(lane == i1, neg, logits)
    m2 = jnp.max(rest, axis=-1, keepdims=True)
    i2 = jnp.min(jnp.where(rest == m2, lane, LANE), axis=-1, keepdims=True)
    e = jnp.exp(m2 - m1)
    w1 = 1.0 / (1.0 + e)
    w2 = e / (1.0 + e)
    o_ref[...] = (jnp.where(lane == 0, i1.astype(F32), 0.0) + jnp.where(lane == 1, i2.astype(F32), 0.0)
                  + jnp.where(lane == 2, w1, 0.0) + jnp.where(lane == 3, w2, 0.0))


def _router(x2, nw, sc, sh, rw_pad, seq):
    m, d = x2.shape
    tm = ROUTER_TM
    per_b = seq // tm
    return pl.pallas_call(
        _router_kernel,
        grid=(m // tm,),
        in_specs=[pl.BlockSpec((tm, d), lambda i: (i, 0)),
                  pl.BlockSpec((1, d), lambda i: (0, 0)),
                  pl.BlockSpec((1, 1, d), lambda i: (i // per_b, 0, 0)),
                  pl.BlockSpec((1, 1, d), lambda i: (i // per_b, 0, 0)),
                  pl.BlockSpec((d, LANE), lambda i: (0, 0))],
        out_specs=[pl.BlockSpec((tm, LANE), lambda i: (i, 0)), pl.BlockSpec((tm, d), lambda i: (i, 0))],
        out_shape=[jax.ShapeDtypeStruct((m, LANE), F32), jax.ShapeDtypeStruct((m, d), F32)],
        compiler_params=_cparams(("parallel",)),
        name="router",
    )(x2, nw, sc, sh, rw_pad)


MOE_TM = 448
MOE_NF = 7


def _moe_plan(rt, m):
    tm = MOE_TM
    n_assign = TOP_K * m
    nt = -(-n_assign // tm) + N_EXPERTS
    e_all = rt[:, :TOP_K].astype(jnp.int32).reshape(-1)
    onehot = (e_all[:, None] == jnp.arange(N_EXPERTS, dtype=jnp.int32)[None, :]).astype(jnp.int32)
    csum = jnp.cumsum(onehot, axis=0)
    rank = jnp.take_along_axis(csum, e_all[:, None], axis=1)[:, 0] - 1
    counts = csum[-1]
    padded = ((counts + tm - 1) // tm) * tm
    ends = jnp.cumsum(padded)
    dest = (ends - padded)[e_all] + rank
    src = jnp.full((nt * tm,), -1, jnp.int32).at[dest].set(jnp.arange(n_assign, dtype=jnp.int32))
    tile_expert = jnp.minimum(jnp.searchsorted(ends, jnp.arange(nt, dtype=jnp.int32) * tm, side="right"),
                              N_EXPERTS - 1).astype(jnp.int32)
    src = src.reshape(nt, tm)
    valid = src >= 0
    gather_tok = jnp.where(valid, src // TOP_K, 0)
    dump_row = m + jnp.arange(tm, dtype=jnp.int32)[None, :]
    tile_par = (jnp.arange(nt, dtype=jnp.int32) % 2)[:, None]
    srow = jnp.concatenate([dump_row, jnp.where(valid, src // TOP_K, dump_row)], axis=0)
    scol = jnp.concatenate([jnp.ones((1, tm), jnp.int32), jnp.where(valid, src % TOP_K, tile_par)], axis=0)
    return (gather_tok.reshape(nt, 1, tm), srow.reshape(nt + 1, 1, tm), scol.reshape(nt + 1, 1, tm), tile_expert)


def _moe_kernel(te_ref, srow_ref, scol_ref, cur_ref, nxt_ref, h_hbm, w1_ref, w3_ref, w2_ref, y_hbm,
                gbuf, obuf, hb_ref, acc_ref, gsem, ssem):
    t = pl.program_id(0)
    j = pl.program_id(1)
    nt = pl.num_programs(0) - 1
    tm = MOE_TM
    d = hb_ref.shape[1]
    per_step = tm // MOE_NF
    slot = t % 2

    def gather_copy(idx_ref, r, sl):
        return pltpu.make_async_copy(h_hbm.at[pl.ds(idx_ref[0, 0, r], 1)], gbuf.at[sl, pl.ds(r, 1)], gsem.at[sl])

    def scatter_copy(r, sl):
        col = pl.multiple_of(scol_ref[0, 0, r] * d, d)
        return pltpu.make_async_copy(obuf.at[sl, pl.ds(r, 1)], y_hbm.at[pl.ds(srow_ref[0, 0, r], 1), pl.ds(col, d)],
                                     ssem.at[sl])

    def wait_gather(sl):
        pltpu.make_async_copy(gbuf.at[sl], gbuf.at[sl], gsem.at[sl]).wait()

    def wait_scatter(sl):
        pltpu.make_async_copy(obuf.at[sl], obuf.at[sl], ssem.at[sl]).wait()

    @pl.when((t == 0) & (j == 0))
    def _():
        def body(r, c):
            gather_copy(cur_ref, r, 0).start()
            return c
        lax.fori_loop(0, tm, body, 0)
        obuf[1] = jnp.zeros((tm, d), F32)

    @pl.when(j == 0)
    def _():
        wait_gather(slot)
        hb_ref[...] = gbuf[slot].astype(BF16)
        acc_ref[...] = jnp.zeros_like(acc_ref)

    base = j * per_step
    for i in range(per_step):
        gather_copy(nxt_ref, base + i, 1 - slot).start()
        scatter_copy(base + i, 1 - slot).start()
    h = hb_ref[...]
    a = _dot(h, w1_ref[0])
    b = _dot(h, w3_ref[0])
    acc_ref[...] += _dot((_silu(a) * b).astype(BF16), w2_ref[0])

    @pl.when(j == MOE_NF - 1)
    def _():
        @pl.when(t > 0)
        def _():
            wait_scatter(slot)

        obuf[slot] = acc_ref[...]

        @pl.when(t == nt)
        def _():
            wait_scatter(1 - slot)
            wait_gather(1 - slot)


def _moe(h, gather_tok, srow, scol, tile_expert, w1, w3, w2):
    m, d = h.shape
    ne, _, f = w1.shape
    tm, tf = MOE_TM, f // MOE_NF
    nt = gather_tok.shape[0]

    def smem_tile(idx):
        return pl.BlockSpec((1, 1, tm), idx, memory_space=pltpu.SMEM)

    def tile(t):
        return jnp.minimum(t, nt - 1)

    grid_spec = pltpu.PrefetchScalarGridSpec(
        num_scalar_prefetch=1,
        grid=(nt + 1, MOE_NF),
        in_specs=[smem_tile(lambda t, j, te: (t, 0, 0)),
                  smem_tile(lambda t, j, te: (t, 0, 0)),
                  smem_tile(lambda t, j, te: (tile(t), 0, 0)),
                  smem_tile(lambda t, j, te: (tile(t + 1), 0, 0)),
                  pl.BlockSpec(memory_space=pl.ANY),
                  pl.BlockSpec((1, d, tf), lambda t, j, te: (te[tile(t)], 0, j)),
                  pl.BlockSpec((1, d, tf), lambda t, j, te: (te[tile(t)], 0, j)),
                  pl.BlockSpec((1, tf, d), lambda t, j, te: (te[tile(t)], j, 0))],
        out_specs=pl.BlockSpec(memory_space=pl.ANY),
        scratch_shapes=[pltpu.VMEM((2, tm, d), F32), pltpu.VMEM((2, tm, d), F32), pltpu.VMEM((tm, d), BF16),
                        pltpu.VMEM((tm, d), F32), pltpu.SemaphoreType.DMA((2,)), pltpu.SemaphoreType.DMA((2,))])
    return pl.pallas_call(
        _moe_kernel,
        grid_spec=grid_spec,
        out_shape=jax.ShapeDtypeStruct((m + tm, TOP_K * d), F32),
        compiler_params=_cparams(("arbitrary", "arbitrary")),
        name="moe",
    )(tile_expert, srow, scol, gather_tok, gather_tok, h, w1, w3, w2)


COMBINE_TM = 512


def _combine_kernel(y_ref, rt_ref, x_ref, g2_ref, nwo_ref, o_ref):
    d = x_ref.shape[1]
    rt = rt_ref[...]
    y = rt[:, 2:3] * y_ref[:, :d] + rt[:, 3:4] * y_ref[:, d:]
    o_ref[...] = x_ref[...] + g2_ref[0] * _rms_rows(y, nwo_ref[...])


def _combine(y2, rt, x2, g2, nwo, seq):
    m, d = x2.shape
    tm = COMBINE_TM
    per_b = seq // tm
    return pl.pallas_call(
        _combine_kernel,
        grid=(m // tm,),
        in_specs=[pl.BlockSpec((tm, TOP_K * d), lambda i: (i, 0)),
                  pl.BlockSpec((tm, LANE), lambda i: (i, 0)),
                  pl.BlockSpec((tm, d), lambda i: (i, 0)),
                  pl.BlockSpec((1, 1, d), lambda i: (i // per_b, 0, 0)),
                  pl.BlockSpec((1, d), lambda i: (0, 0))],
        out_specs=pl.BlockSpec((tm, d), lambda i: (i, 0)),
        out_shape=jax.ShapeDtypeStruct((m, d), F32),
        compiler_params=_cparams(("parallel",)),
        name="moe_combine",
    )(y2, rt, x2, g2, nwo)


def _reorder_w_in(w):
    o = _IN_OFF

    def cols(a, b):
        return w[:, o[a]:o[b]]

    small = jnp.concatenate([cols(4, 6), cols(13, 17)], axis=1)
    pad = jnp.zeros((w.shape[0], P_COLS - P_SMALL - small.shape[1]), w.dtype)
    out = jnp.concatenate([cols(6, 9),
                           cols(2, 4),
                           cols(0, 2),
                           cols(9, 13),
                           cols(17, 18),
                           small, pad], axis=1)
    return out.astype(BF16)


def _gla_gate_params(up, bias):
    hi = up.astype(BF16)
    lo = (up - hi.astype(F32)).astype(BF16)
    out = jnp.zeros((2, 2 * LANE, GLA_HEADS * GLA_DK), BF16)
    for dd, off in ((0, S_GWF), (1, S_GWB)):
        out = out.at[dd, off:off + GLA_RANK].set(hi[dd])
        out = out.at[dd, LANE + off:LANE + off + GLA_RANK].set(lo[dd])
    return out, bias.reshape(2, 1, -1).astype(F32)


def _gdn_gate_params(a_log, dt_bias):
    h = GDN_HEADS
    sel = np.zeros((2, LANE, LANE), np.float32)
    for dd, (boff, aoff) in enumerate(((S_NBF, S_NAF), (S_NBB, S_NAB))):
        for i in range(h):
            sel[dd, boff + i, i] = 1.0
            sel[dd, aoff + i, h + i] = 1.0
    nega = jnp.zeros((2, 1, LANE), F32).at[:, 0, h:2 * h].set(-jnp.exp(a_log.astype(F32)))
    dtb = jnp.zeros((2, 1, LANE), F32).at[:, 0, h:2 * h].set(dt_bias.astype(F32))
    return jnp.asarray(sel, BF16), nega, dtb


def _rope_inv_row():
    inv = ROPE_THETA ** (-jnp.arange(0, ROT_DIM, 2, dtype=F32) / ROT_DIM)
    lane = np.arange(LANE) % DIFF_DH
    row = jnp.where(jnp.asarray(lane < ROT_DIM), inv[jnp.asarray(lane % (ROT_DIM // 2))], 0.0)
    return row.reshape(1, LANE).astype(F32)


def kernel(x, c, positions, adaln_w, adaln_b, norm_w, w_in, gla_gate_up, gla_gate_bias, gla_norm, diff_lambda,
           diff_norm, gdn_conv, gdn_A_log, gdn_dt_bias, gdn_norm, w_branch, w_out, ffn_w1, ffn_w3, ffn_w2,
           router_w, moe_w1, moe_w3, moe_w2):
    batch, seq, d = x.shape
    depth = adaln_w.shape[0]
    m = batch * seq

    c_pad = jnp.zeros((8, d), F32).at[:batch].set(c)
    mod = _adaln(c_pad, adaln_w, adaln_b)[:, :batch]
    pos3 = positions.reshape(batch, seq, 1)
    inv_row = _rope_inv_row()

    for layer in range(depth):
        sh1, sc1, g1, sh2, sc2, g2 = [t.reshape(batch, 1, d) for t in jnp.split(mod[layer], 6, axis=-1)]
        nw = norm_w[layer].astype(F32)

        p = _inproj(x.reshape(m, d), nw[0:1], sc1, sh1, _reorder_w_in(w_in[layer]), seq)
        p3 = p.reshape(batch, seq, P_COLS)

        up2, gbias = _gla_gate_params(gla_gate_up[layer], gla_gate_bias[layer])
        gla_o = _gla(p3, up2, gbias, batch, seq)

        qk = _rope(p3, pos3, inv_row, batch, seq)
        lam_init = 0.8 - 0.6 * math.exp(-0.3 * layer)
        att_o = _attn(qk, p3, diff_lambda[layer].astype(F32), lam_init, batch, seq)

        qkv = _gdn_prep(p3, gdn_conv[layer].astype(F32), batch, seq)
        sel_m, nega, dtb = _gdn_gate_params(gdn_A_log[layer], gdn_dt_bias[layer])
        gdn_o = _gdn(qkv, p3, sel_m, nega, dtb, batch, seq)

        nrm3 = jnp.stack([jnp.tile(gla_norm[layer], 4), jnp.tile(diff_norm[layer], 4),
                          jnp.tile(gdn_norm[layer], 4)]).astype(F32)
        x = _merge(gla_o, p3, att_o, gdn_o, x, g1, nrm3, w_branch[layer].astype(BF16),
                   w_out[layer].astype(BF16), nw[1:2], 1.0 - lam_init, batch, seq)

        x2 = x.reshape(m, d)
        if layer % 2 == 0:
            li = layer // 2
            x2 = _ffn(x2, nw[2:3], sc2, sh2, ffn_w1[li].astype(BF16), ffn_w3[li].astype(BF16),
                      ffn_w2[li].astype(BF16), g2, nw[3:4], seq)
        else:
            li = layer // 2
            rw_pad = jnp.zeros((d, LANE), F32).at[:, :N_EXPERTS].set(router_w[li])
            rt, h2 = _router(x2, nw[2:3], sc2, sh2, rw_pad, seq)
            y = _moe(h2, *_moe_plan(rt, m), moe_w1[li].astype(BF16), moe_w3[li].astype(BF16),
                     moe_w2[li].astype(BF16))
            x2 = _combine(y, rt, x2, g2, nw[3:4], seq)
        x = x2.reshape(batch, seq, d)
    return x
```

```python
import functools
import math

import jax
import jax.numpy as jnp
import numpy as np
from jax import lax
from jax.experimental import pallas as pl
from jax.experimental.pallas import tpu as pltpu

F32 = jnp.float32
BF16 = jnp.bfloat16

D_MODEL = 1024
GLA_HEADS, GLA_DK, GLA_DV, GLA_RANK, GLA_TAU = 4, 64, 128, 16, 16.0
DIFF_HEADS, DIFF_DH = 4, 64
ROT_DIM = DIFF_DH // 4
ROPE_THETA = 500000.0
GDN_HEADS, GDN_DK, GDN_DV, CONV_W = 4, 128, 128, 4
CHUNK = 64
N_BRANCH = 3
N_EXPERTS, TOP_K = 8, 2
EPS = 1e-6
BRANCH_W = 512

_IN_SIZES = (256, 256, 512, 512, 16, 16, 512, 512, 512, 512, 512, 512, 512, 4, 4, 4, 4, 3 * D_MODEL)
_IN_OFF = tuple(int(v) for v in np.concatenate([[0], np.cumsum(_IN_SIZES)]))

P_DQ, P_DK, P_DV = 0, 512, 1024
P_GV, P_GR = 1536, 2048
P_GQ, P_GK = 2560, 2816
P_NQ, P_NK, P_NV, P_NZ = 3072, 3584, 4096, 4608
P_MG = 5120
P_SMALL = 8192
P_COLS = 8448
S_GWF, S_GWB, S_NBF, S_NBB, S_NAF, S_NAB = 0, 16, 32, 36, 40, 44

VMEM_LIMIT = 56 * 1024 * 1024
LANE = 128


def _cparams(sem):
    return pltpu.CompilerParams(dimension_semantics=sem, vmem_limit_bytes=VMEM_LIMIT)


def _dot(a, b):
    return jnp.dot(a, b, preferred_element_type=F32)


def _dot_nt(a, b):
    return lax.dot_general(a, b, (((1,), (1,)), ((), ())), preferred_element_type=F32)


def _dot_tn(a, b):
    return lax.dot_general(a, b, (((0,), (0,)), ((), ())), preferred_element_type=F32)


def _split3_rows(x):
    hi = x.astype(BF16)
    r1 = x - hi.astype(F32)
    mid = r1.astype(BF16)
    lo = (r1 - mid.astype(F32)).astype(BF16)
    return jnp.concatenate([hi, mid, lo], axis=0)


def _dot_split(a, b):
    a_hi = a.astype(BF16)
    a_lo = (a - a_hi.astype(F32)).astype(BF16)
    b_hi = b.astype(BF16)
    b_lo = (b - b_hi.astype(F32)).astype(BF16)
    return _dot(jnp.concatenate([a_hi, a_lo, a_hi], axis=1), jnp.concatenate([b_hi, b_hi, b_lo], axis=0))


def _sigmoid(x):
    return 1.0 / (1.0 + jnp.exp(-x))


def _silu(x):
    return x * _sigmoid(x)


def _softplus(x):
    return jnp.maximum(x, 0.0) + jnp.log1p(jnp.exp(-jnp.abs(x)))


def _rms_rows(x, w):
    return x * lax.rsqrt(jnp.mean(x * x, axis=-1, keepdims=True) + EPS) * w


def _tri_masks(d):
    r = lax.broadcasted_iota(jnp.int32, (CHUNK, CHUNK), 0)
    c = lax.broadcasted_iota(jnp.int32, (CHUNK, CHUNK), 1)
    lag = (r - c) * (1 - 2 * d)
    return lag >= 0, lag > 0


def _adaln_kernel(c_ref, w_ref, b_ref, o_ref):
    c = c_ref[...]
    o_ref[0] = jnp.dot(_silu(c), w_ref[0], preferred_element_type=F32,
                       precision=lax.Precision.HIGHEST) + b_ref[0]


def _adaln(c_pad, adaln_w, adaln_b):
    depth, d, n = adaln_w.shape
    tn = 1536
    return pl.pallas_call(
        _adaln_kernel,
        grid=(depth, n // tn),
        in_specs=[pl.BlockSpec((8, d), lambda l, j: (0, 0)),
                  pl.BlockSpec((1, d, tn), lambda l, j: (l, 0, j)),
                  pl.BlockSpec((1, 1, tn), lambda l, j: (l, 0, j))],
        out_specs=pl.BlockSpec((1, 8, tn), lambda l, j: (l, 0, j)),
        out_shape=jax.ShapeDtypeStruct((depth, 8, n), F32),
        compiler_params=_cparams(("parallel", "parallel")),
        name="adaln",
    )(c_pad, adaln_w, adaln_b.reshape(depth, 1, n))


def _inproj_kernel(x_ref, nw_ref, sc_ref, sh_ref, w_ref, o_ref):
    h = (_rms_rows(x_ref[...], nw_ref[...]) * (1.0 + sc_ref[0]) + sh_ref[0]).astype(BF16)
    o_ref[...] = _dot(h, w_ref[...]).astype(o_ref.dtype)


def _resident(shape):
    return pl.BlockSpec(shape, lambda *_: (0,) * len(shape), pipeline_mode=pl.Buffered(1))


def _inproj(x2, nw, sc, sh, w, seq):
    m, d = x2.shape
    n = w.shape[1]
    tm = 512
    per_b = seq // tm
    return pl.pallas_call(
        _inproj_kernel,
        grid=(m // tm,),
        in_specs=[pl.BlockSpec((tm, d), lambda i: (i, 0)),
                  pl.BlockSpec((1, d), lambda i: (0, 0)),
                  pl.BlockSpec((1, 1, d), lambda i: (i // per_b, 0, 0)),
                  pl.BlockSpec((1, 1, d), lambda i: (i // per_b, 0, 0)),
                  _resident((d, n))],
        out_specs=pl.BlockSpec((tm, n), lambda i: (i, 0)),
        out_shape=jax.ShapeDtypeStruct((m, n), BF16),
        compiler_params=_cparams(("parallel",)),
        name="inproj",
    )(x2, nw, sc, sh, w)


GLA_TL = 512
GLA_CB = 8


def _gla_kernel(q_ref, k_ref, v_ref, sm_ref, up_ref, bias_ref, o_ref, st_ref):
    d = pl.program_id(0)
    nc = GLA_TL // CHUNK

    @pl.when(pl.program_id(2) == 0)
    def _():
        st_ref[...] = jnp.zeros_like(st_ref)

    incl, _ = _tri_masks(d)
    tri = jnp.where(incl, 1.0, 0.0).astype(BF16)
    tri3 = jnp.concatenate([tri, tri, tri], axis=1)
    fwd = d == 0
    scale = GLA_DK ** -0.5

    H = GLA_HEADS
    ks = [slice(h * GLA_DK, (h + 1) * GLA_DK) for h in range(H)]
    vs = [slice(h * GLA_DV, (h + 1) * GLA_DV) for h in range(H)]

    def group(gi, carry):
        rows = []
        for ci in range(GLA_CB):
            c = gi * GLA_CB + ci
            ce = jnp.where(fwd, c, nc - 1 - c)
            rows.append(pl.ds(pl.multiple_of(ce * CHUNK, CHUNK), CHUNK))
        cr = range(GLA_CB)
        sm = [sm_ref[0, r, :] for r in rows]
        z = [_dot(jnp.concatenate([x, x], axis=1), up_ref[0]) + bias_ref[0] for x in sm]
        g = [(jnp.minimum(x, 0.0) - jnp.log1p(jnp.exp(-jnp.abs(x)))) * (1.0 / GLA_TAU) for x in z]
        b = [_dot(tri3, _split3_rows(x)) for x in g]
        b_last = [jnp.where(fwd, x[CHUNK - 1:CHUNK], x[0:1]) for x in b]
        b_mid = [jnp.where(fwd, x[CHUNK // 2 - 1:CHUNK // 2], x[CHUNK // 2:CHUNK // 2 + 1]) for x in b]
        q = [q_ref[0, r, :].astype(F32) * scale for r in rows]
        k = [k_ref[0, r, :].astype(F32) for r in rows]
        qe = [(q[i] * jnp.exp(b[i] - b_mid[i])).astype(BF16) for i in cr]
        ke = [(k[i] * jnp.exp(b_mid[i] - b[i])).astype(BF16) for i in cr]
        kend = [(k[i] * jnp.exp(b_last[i] - b[i])).astype(BF16) for i in cr]
        qd = [(q[i] * jnp.exp(b[i])).astype(BF16) for i in cr]
        dec = [jnp.exp(x) for x in b_last]
        ch = [(ci, h) for ci in cr for h in range(H)]
        v_a = [v_ref[0, rows[ci], vs[h]] for ci, h in ch]
        a = [jnp.where(incl, _dot_nt(qe[ci][:, ks[h]], ke[ci][:, ks[h]]), 0.0).astype(BF16) for ci, h in ch]
        o_in = [_dot(a[i], v_a[i]) for i in range(len(ch))]
        kv = [_dot_tn(v_a[i], kend[ci][:, ks[h]]) for i, (ci, h) in enumerate(ch)]
        st = st_ref[...]
        for ci in cr:
            st_b = st.astype(BF16)
            for h in range(H):
                o = o_in[ci * H + h] + _dot_nt(qd[ci][:, ks[h]], st_b[:, ks[h]])
                o_ref[0, 0, rows[ci], vs[h]] = o.astype(o_ref.dtype)
            st = st * dec[ci] + jnp.concatenate([kv[ci * H + h] for h in range(H)], axis=1)
        st_ref[...] = st
        return carry

    lax.fori_loop(0, nc // GLA_CB, group, 0)


def _gla(p3, up2, bias, batch, seq):
    nblk = seq // GLA_TL

    def blk(dd, j):
        return j + dd * (nblk - 1 - 2 * j)

    return pl.pallas_call(
        _gla_kernel,
        grid=(2, batch, nblk),
        in_specs=[pl.BlockSpec((1, GLA_TL, 256), lambda dd, b, j: (b, blk(dd, j), P_GQ // 256)),
                  pl.BlockSpec((1, GLA_TL, 256), lambda dd, b, j: (b, blk(dd, j), P_GK // 256)),
                  pl.BlockSpec((1, GLA_TL, 512), lambda dd, b, j: (b, blk(dd, j), P_GV // 512)),
                  pl.BlockSpec((1, GLA_TL, LANE), lambda dd, b, j: (b, blk(dd, j), P_SMALL // LANE)),
                  pl.BlockSpec((1, 2 * LANE, 256), lambda dd, b, j: (dd, 0, 0)),
                  pl.BlockSpec((1, 1, 256), lambda dd, b, j: (dd, 0, 0))],
        out_specs=pl.BlockSpec((1, 1, GLA_TL, 512), lambda dd, b, j: (dd, b, blk(dd, j), 0)),
        out_shape=jax.ShapeDtypeStruct((2, batch, seq, 512), BF16),
        scratch_shapes=[pltpu.VMEM((GLA_DV, GLA_HEADS * GLA_DK), F32)],
        compiler_params=_cparams(("parallel", "parallel", "arbitrary")),
        name="gla",
    )(p3, p3, p3, p3, up2, bias)


ROPE_TL = 1024


def _rope_kernel(q_ref, k_ref, pos_ref, inv_ref, o_ref):
    ang = pos_ref[0].astype(F32) * inv_ref[...]
    cosf = jnp.cos(ang)
    sinf = jnp.sin(ang)
    lane = lax.broadcasted_iota(jnp.int32, (1, LANE), 1) % DIFF_DH
    half = ROT_DIM // 2
    cmul = jnp.where(lane < ROT_DIM, cosf, 1.0)
    s_up = jnp.where(lane < half, -sinf, 0.0)
    s_dn = jnp.where((lane >= half) & (lane < ROT_DIM), sinf, 0.0)
    for s, (x_ref, scale) in enumerate(((q_ref, DIFF_DH ** -0.5 * math.log2(math.e)), (k_ref, 1.0))):
        for gidx in range(4):
            cs = slice(gidx * LANE, (gidx + 1) * LANE)
            x = x_ref[0, :, cs].astype(F32)
            y = x * cmul + pltpu.roll(x, LANE - half, 1) * s_up + pltpu.roll(x, half, 1) * s_dn
            o_ref[s, 0, :, cs] = (y * scale).astype(o_ref.dtype)


def _rope(p3, pos3, inv_row, batch, seq):
    nblk = seq // ROPE_TL
    return pl.pallas_call(
        _rope_kernel,
        grid=(batch, nblk),
        in_specs=[pl.BlockSpec((1, ROPE_TL, 512), lambda b, j: (b, j, P_DQ // 512)),
                  pl.BlockSpec((1, ROPE_TL, 512), lambda b, j: (b, j, P_DK // 512)),
                  pl.BlockSpec((1, ROPE_TL, 1), lambda b, j: (b, j, 0)),
                  pl.BlockSpec((1, LANE), lambda b, j: (0, 0))],
        out_specs=pl.BlockSpec((2, 1, ROPE_TL, 512), lambda b, j: (0, b, j, 0)),
        out_shape=jax.ShapeDtypeStruct((2, batch, seq, 512), BF16),
        compiler_params=_cparams(("parallel", "parallel")),
        name="rope",
    )(p3, p3, pos3, inv_row)


ATT_TQ = 512
ATT_TK = 2048


def _attn_kernel(q_ref, k_ref, v_ref, lam_ref, o_ref, sa_ref, sb_ref, *, lam_init):
    q = q_ref[0, 0]
    lane = lax.broadcasted_iota(jnp.int32, q.shape, 1)
    zero = jnp.zeros_like(q)
    q1 = jnp.where(lane < DIFF_DH, q, zero)
    q2 = jnp.where(lane >= DIFF_DH, q, zero)
    tq = q.shape[0]
    tk = sa_ref.shape[2]
    nk = k_ref.shape[2] // tk
    ones_col = jnp.where(lax.broadcasted_iota(jnp.int32, (tk, LANE), 1) == 0, 1.0, 0.0).astype(BF16)

    def soft(s, v1, m, l, acc):
        m_new = jnp.maximum(m, jnp.max(s, axis=-1, keepdims=True))
        p = jnp.exp2((s - m_new).astype(BF16))
        alpha = jnp.exp2(m - m_new)
        pv = _dot(p, v1)
        return m_new, alpha * l + pv[:, LANE:LANE + 1], alpha * acc + pv[:, :LANE]

    def k_tile(t):
        return k_ref[0, 0, pl.ds(t * tk, tk), :]

    def v_tile(t):
        return jnp.concatenate([v_ref[0, pl.ds(t * tk, tk), :], ones_col], axis=1)

    def scores(t, dst):
        k = k_tile(t)
        dst[0] = _dot_nt(q1, k)
        dst[1] = _dot_nt(q2, k)

    def consume(t, src, carry):
        m1, l1, a1, m2, l2, a2 = carry
        v1 = v_tile(t)
        m1, l1, a1 = soft(src[0], v1, m1, l1, a1)
        m2, l2, a2 = soft(src[1], v1, m2, l2, a2)
        return m1, l1, a1, m2, l2, a2

    neg = jnp.full((tq, 1), -jnp.inf, F32)
    zl = jnp.zeros((tq, 1), F32)
    za = jnp.zeros((tq, LANE), F32)
    carry = (neg, zl, za, neg, zl, za)
    bufs = (sa_ref, sb_ref)
    scores(0, bufs[0])
    for t in range(nk):
        if t + 1 < nk:
            scores(t + 1, bufs[(t + 1) % 2])
        carry = consume(t, bufs[t % 2], carry)
    m1, l1, a1, m2, l2, a2 = carry
    lp = lam_ref[...]
    lam = (jnp.exp(jnp.sum(lp[0:1] * lp[1:2], axis=-1, keepdims=True))
           - jnp.exp(jnp.sum(lp[2:3] * lp[3:4], axis=-1, keepdims=True)) + lam_init)
    o_ref[0] = (a1 / l1 - lam * (a2 / l2)).astype(o_ref.dtype)


def _attn(qk, p3, lam_p, lam_init, batch, seq):
    nq = seq // ATT_TQ
    tk = min(ATT_TK, seq // 2)
    return pl.pallas_call(
        functools.partial(_attn_kernel, lam_init=lam_init),
        grid=(batch, DIFF_HEADS, nq),
        in_specs=[pl.BlockSpec((1, 1, ATT_TQ, LANE), lambda b, h, i: (0, b, i, h)),
                  pl.BlockSpec((1, 1, seq, LANE), lambda b, h, i: (1, b, 0, h)),
                  pl.BlockSpec((1, seq, LANE), lambda b, h, i: (b, 0, P_DV // LANE + h)),
                  pl.BlockSpec((4, DIFF_DH), lambda b, h, i: (0, 0))],
        out_specs=pl.BlockSpec((1, ATT_TQ, LANE), lambda b, h, i: (b, i, h)),
        out_shape=jax.ShapeDtypeStruct((batch, seq, 512), BF16),
        scratch_shapes=[pltpu.VMEM((2, ATT_TQ, tk), F32), pltpu.VMEM((2, ATT_TQ, tk), F32)],
        compiler_params=_cparams(("parallel", "parallel", "arbitrary")),
        name="diffattn",
    )(qk, qk, p3, lam_p)


GPREP_TL = 1024
HALO = 16


def _gdn_prep_kernel(x_ref, prev_ref, next_ref, w_ref, o_ref, xe_ref):
    j = pl.program_id(1)
    s = pl.program_id(2)
    nblk = pl.num_programs(1)
    tl = GPREP_TL
    prev = prev_ref[0].astype(F32)
    nxt = next_ref[0].astype(F32)
    xe_ref[0:HALO, :] = jnp.where(j == 0, 0.0, prev)
    xe_ref[HALO:HALO + tl, :] = x_ref[0].astype(F32)
    xe_ref[HALO + tl:2 * HALO + tl, :] = jnp.where(j == nblk - 1, 0.0, nxt)
    w = w_ref[...]
    y = (xe_ref[pl.ds(HALO - 1, tl), :] * w[0:1] + xe_ref[pl.ds(HALO, tl), :] * w[1:2]
         + xe_ref[pl.ds(HALO + 1, tl), :] * w[2:3] + xe_ref[pl.ds(HALO + 2, tl), :] * w[3:4])
    y = _silu(y)
    qscale = jnp.where(s == 0, GDN_DK ** -0.5, 1.0)
    for h in range(GDN_HEADS):
        cs = slice(h * GDN_DK, (h + 1) * GDN_DK)
        yh = y[:, cs]
        nrm = lax.rsqrt(jnp.sum(yh * yh, axis=-1, keepdims=True) + EPS) * qscale
        fac = jnp.where(s == 2, 1.0, nrm)
        o_ref[0, 0, :, cs] = (yh * fac).astype(o_ref.dtype)


def _gdn_prep(p3, conv_w, batch, seq):
    nblk = seq // GPREP_TL
    hb = GPREP_TL // HALO
    last = seq // HALO - 1
    base = P_NQ // 512
    return pl.pallas_call(
        _gdn_prep_kernel,
        grid=(batch, nblk, 3),
        in_specs=[pl.BlockSpec((1, GPREP_TL, 512), lambda b, j, s: (b, j, base + s)),
                  pl.BlockSpec((1, HALO, 512), lambda b, j, s: (b, jnp.maximum(j * hb - 1, 0), base + s)),
                  pl.BlockSpec((1, HALO, 512), lambda b, j, s: (b, jnp.minimum((j + 1) * hb, last), base + s)),
                  pl.BlockSpec((CONV_W, 512), lambda b, j, s: (0, s))],
        out_specs=pl.BlockSpec((1, 1, GPREP_TL, 512), lambda b, j, s: (s, b, j, 0)),
        out_shape=jax.ShapeDtypeStruct((3, batch, seq, 512), BF16),
        scratch_shapes=[pltpu.VMEM((GPREP_TL + 2 * HALO, 512), F32)],
        compiler_params=_cparams(("parallel", "parallel", "arbitrary")),
        name="gdn_prep",
    )(p3, p3, p3, conv_w)


GDN_TL = 512
GDN_CB = 4


def _gdn_kernel(q_ref, k_ref, v_ref, sm_ref, sel_ref, nega_ref, dtb_ref, o_ref, s_ref):
    d = pl.program_id(0)
    nc = GDN_TL // CHUNK

    @pl.when(pl.program_id(2) == 0)
    def _():
        s_ref[...] = jnp.zeros_like(s_ref)

    incl, strict = _tri_masks(d)
    tri = jnp.where(incl, 1.0, 0.0).astype(BF16)
    tri3 = jnp.concatenate([tri, tri, tri], axis=1)
    upper = jnp.where(strict, 1.0, 0.0)
    r_i = lax.broadcasted_iota(jnp.int32, (CHUNK, CHUNK), 0)
    c_i = lax.broadcasted_iota(jnp.int32, (CHUNK, CHUNK), 1)
    eye = jnp.where(r_i == c_i, 1.0, 0.0)
    fwd = d == 0
    H = GDN_HEADS

    def group(gi, carry):
        rows = []
        for ci in range(GDN_CB):
            c = gi * GDN_CB + ci
            ce = jnp.where(fwd, c, nc - 1 - c)
            rows.append(pl.ds(pl.multiple_of(ce * CHUNK, CHUNK), CHUNK))
        sel = [_dot(sm_ref[0, r, :], sel_ref[0]) for r in rows]
        beta = [_sigmoid(x) for x in sel]
        g = [nega_ref[0] * _softplus(x + dtb_ref[0]) for x in sel]
        g2 = [jnp.concatenate([x[:, H + h:H + h + 1] * upper for h in range(H)] + [x], axis=1) for x in g]
        dd = [_dot(tri3, _split3_rows(x)) for x in g2]
        ch = [(ci, h) for ci in range(GDN_CB) for h in range(H)]
        hs = [slice(h * GDN_DK, (h + 1) * GDN_DK) for h in range(H)]
        k_a = [k_ref[0, 0, rows[ci], hs[h]] for ci, h in ch]
        q_a = [q_ref[0, 0, rows[ci], hs[h]] for ci, h in ch]
        v_a = [v_ref[0, 0, rows[ci], hs[h]] for ci, h in ch]
        n_ch = range(len(ch))
        kf = [k_a[i].astype(F32) for i in n_ch]
        gam = [jnp.where(incl, jnp.exp(dd[ci][:, h * CHUNK:(h + 1) * CHUNK]), 0.0) for ci, h in ch]
        bcol = [dd[ci][:, H * CHUNK + H + h:H * CHUNK + H + h + 1] for ci, h in ch]
        blast = [jnp.where(fwd, b[CHUNK - 1:CHUNK], b[0:1]) for b in bcol]
        eb = [jnp.exp(b) for b in bcol]
        bet = [beta[ci][:, h:h + 1] for ci, h in ch]
        kk = [_dot_nt(k_a[i], k_a[i]) for i in n_ch]
        qk = [_dot_nt(q_a[i], k_a[i]) for i in n_ch]
        pw = [jnp.where(strict, -(bet[i] * kk[i] * gam[i]), 0.0) for i in n_ch]
        t = [eye + pw[i] for i in n_ch]
        for _ in range(5):
            pw = [_dot_split(pw[i], pw[i]) for i in n_ch]
            t = [t[i] + _dot_split(t[i], pw[i]) for i in n_ch]
        rhs = [jnp.concatenate([v_a[i].astype(F32) * bet[i], kf[i] * (bet[i] * eb[i])], axis=1).astype(BF16)
               for i in n_ch]
        sol = [_dot(t[i].astype(BF16), rhs[i]) for i in n_ch]
        u = [sol[i][:, :GDN_DV] for i in n_ch]
        w = [sol[i][:, GDN_DV:].astype(BF16) for i in n_ch]
        attn = [(qk[i] * gam[i]).astype(BF16) for i in n_ch]
        qdec = [(q_a[i].astype(F32) * eb[i]).astype(BF16) for i in n_ch]
        kend = [(kf[i] * jnp.exp(blast[i] - bcol[i])).astype(BF16) for i in n_ch]
        dec = [jnp.exp(b) for b in blast]
        st = [s_ref[h] for h in range(H)]
        for ci in range(GDN_CB):
            ix = [ci * H + h for h in range(H)]
            st_b = [st[h].astype(BF16) for h in range(H)]
            vnb = [(u[i] - _dot(w[i], st_b[h])).astype(BF16) for h, i in enumerate(ix)]
            o_new = [(_dot(qdec[i], st_b[h]) + _dot(attn[i], vnb[h])).astype(o_ref.dtype) for h, i in enumerate(ix)]
            st = [dec[i] * st[h] + _dot_tn(kend[i], vnb[h]) for h, i in enumerate(ix)]
            for h in range(H):
                o_ref[0, 0, rows[ci], hs[h]] = o_new[h]
        for h in range(H):
            s_ref[h] = st[h]
        return carry

    lax.fori_loop(0, nc // GDN_CB, group, 0)


def _gdn(qkv, p3, sel_m, nega, dtb, batch, seq):
    nblk = seq // GDN_TL

    def blk(dd, j):
        return j + dd * (nblk - 1 - 2 * j)

    def qspec(idx):
        return pl.BlockSpec((1, 1, GDN_TL, 512), lambda dd, b, j: (idx, b, blk(dd, j), 0))

    return pl.pallas_call(
        _gdn_kernel,
        grid=(2, batch, nblk),
        in_specs=[qspec(0), qspec(1), qspec(2),
                  pl.BlockSpec((1, GDN_TL, LANE), lambda dd, b, j: (b, blk(dd, j), P_SMALL // LANE)),
                  pl.BlockSpec((1, LANE, LANE), lambda dd, b, j: (dd, 0, 0)),
                  pl.BlockSpec((1, 1, LANE), lambda dd, b, j: (dd, 0, 0)),
                  pl.BlockSpec((1, 1, LANE), lambda dd, b, j: (dd, 0, 0))],
        out_specs=pl.BlockSpec((1, 1, GDN_TL, 512), lambda dd, b, j: (dd, b, blk(dd, j), 0)),
        out_shape=jax.ShapeDtypeStruct((2, batch, seq, 512), BF16),
        scratch_shapes=[pltpu.VMEM((GDN_HEADS, GDN_DK, GDN_DV), F32)],
        compiler_params=_cparams(("parallel", "parallel", "arbitrary")),
        name="gdn",
    )(qkv, qkv, qkv, p3, sel_m, nega, dtb)


MERGE_TM = 512


def _head_rms(o, w):
    parts = []
    for h in range(4):
        oh = o[:, h * LANE:(h + 1) * LANE]
        parts.append(oh * lax.rsqrt(jnp.mean(oh * oh, axis=-1, keepdims=True) + EPS))
    return jnp.concatenate(parts, axis=1) * w


def _merge_kernel(gf_ref, gb_ref, gr_ref, od_ref, nf_ref, nb_ref, nz_ref, mg0_ref, mg1_ref, mg2_ref,
                  x_ref, g1_ref, nrm_ref, wb_ref, wo_ref, nw_ref, o_ref, *, diff_scale):
    nrm = nrm_ref[...]
    y_gla = _head_rms(gf_ref[0, 0].astype(F32) + gb_ref[0, 0].astype(F32), nrm[0:1]) * _silu(gr_ref[0].astype(F32))
    y_diff = _head_rms(od_ref[0].astype(F32), nrm[1:2]) * diff_scale
    y_gdn = _head_rms(nf_ref[0, 0].astype(F32) + nb_ref[0, 0].astype(F32), nrm[2:3]) * _silu(nz_ref[0].astype(F32))
    merged = (_sigmoid(mg0_ref[0].astype(F32)) * _dot(y_gla.astype(BF16), wb_ref[0])
              + _sigmoid(mg1_ref[0].astype(F32)) * _dot(y_diff.astype(BF16), wb_ref[1])
              + _sigmoid(mg2_ref[0].astype(F32)) * _dot(y_gdn.astype(BF16), wb_ref[2]))
    y = _dot(merged.astype(BF16), wo_ref[...])
    o_ref[0] = x_ref[0] + g1_ref[0] * _rms_rows(y, nw_ref[...])


def _merge(gla_o, p3, att_o, gdn_o, x3, g1, nrm3, wb, wo, nw, diff_scale, batch, seq):
    tm = MERGE_TM
    nblk = seq // tm
    d = D_MODEL

    def dirspec(idx):
        return pl.BlockSpec((1, 1, tm, 512), lambda b, j: (idx, b, j, 0))

    def pspec(width, col):
        return pl.BlockSpec((1, tm, width), lambda b, j: (b, j, col // width))

    return pl.pallas_call(
        functools.partial(_merge_kernel, diff_scale=diff_scale),
        grid=(batch, nblk),
        in_specs=[dirspec(0), dirspec(1), pspec(512, P_GR),
                  pl.BlockSpec((1, tm, 512), lambda b, j: (b, j, 0)),
                  dirspec(0), dirspec(1), pspec(512, P_NZ),
                  pspec(d, P_MG), pspec(d, P_MG + d), pspec(d, P_MG + 2 * d),
                  pl.BlockSpec((1, tm, d), lambda b, j: (b, j, 0)),
                  pl.BlockSpec((1, 1, d), lambda b, j: (b, 0, 0)),
                  pl.BlockSpec((3, 512), lambda b, j: (0, 0)),
                  pl.BlockSpec((3, 512, d), lambda b, j: (0, 0, 0)),
                  pl.BlockSpec((d, d), lambda b, j: (0, 0)),
                  pl.BlockSpec((1, d), lambda b, j: (0, 0))],
        out_specs=pl.BlockSpec((1, tm, d), lambda b, j: (b, j, 0)),
        out_shape=jax.ShapeDtypeStruct((batch, seq, d), F32),
        compiler_params=_cparams(("parallel", "parallel")),
        name="merge",
    )(gla_o, gla_o, p3, att_o, gdn_o, gdn_o, p3, p3, p3, p3, x3, g1, nrm3, wb, wo, nw)


FFN_TM = 512


def _ffn_kernel(x_ref, nw_ref, sc_ref, sh_ref, w1_ref, w3_ref, w2_ref, g2_ref, nwo_ref, o_ref):
    x = x_ref[...]
    h = (_rms_rows(x, nw_ref[...]) * (1.0 + sc_ref[0]) + sh_ref[0]).astype(BF16)
    a = _dot(h, w1_ref[...])
    b = _dot(h, w3_ref[...])
    y = _dot((_silu(a) * b).astype(BF16), w2_ref[...])
    o_ref[...] = x + g2_ref[0] * _rms_rows(y, nwo_ref[...])


def _ffn(x2, nw, sc, sh, w1, w3, w2, g2, nwo, seq):
    m, d = x2.shape
    f = w1.shape[1]
    tm = FFN_TM
    per_b = seq // tm
    vec = pl.BlockSpec((1, d), lambda i: (0, 0))
    mod = pl.BlockSpec((1, 1, d), lambda i: (i // per_b, 0, 0))
    return pl.pallas_call(
        _ffn_kernel,
        grid=(m // tm,),
        in_specs=[pl.BlockSpec((tm, d), lambda i: (i, 0)), vec, mod, mod,
                  _resident((d, f)), _resident((d, f)), _resident((f, d)), mod, vec],
        out_specs=pl.BlockSpec((tm, d), lambda i: (i, 0)),
        out_shape=jax.ShapeDtypeStruct((m, d), F32),
        compiler_params=_cparams(("parallel",)),
        name="ffn",
    )(x2, nw, sc, sh, w1, w3, w2, g2, nwo)


ROUTER_TM = 1024


def _router_kernel(x_ref, nw_ref, sc_ref, sh_ref, rw_ref, o_ref, h_ref):
    y = _rms_rows(x_ref[...], nw_ref[...])
    h = y * (1.0 + sc_ref[0]) + sh_ref[0]
    h_ref[...] = jnp.swapaxes(jnp.stack([h[:, c * LANE:(c + 1) * LANE] for c in range(h.shape[1] // LANE)], axis=0), 0, 1)
    logits = jnp.dot(h, rw_ref[...], preferred_element_type=F32, precision=lax.Precision.HIGHEST)
    lane = lax.broadcasted_iota(jnp.int32, logits.shape, 1)
    neg = -jnp.inf
    logits = jnp.where(lane < N_EXPERTS, logits, neg)
    m1 = jnp.max(logits, axis=-1, keepdims=True)
    i1 = jnp.min(jnp.where(logits == m1, lane, LANE), axis=-1, keepdims=True)
    rest = jnp.where(lane == i1, neg, logits)
    m2 = jnp.max(rest, axis=-1, keepdims=True)
    i2 = jnp.min(jnp.where(rest == m2, lane, LANE), axis=-1, keepdims=True)
    e = jnp.exp(m2 - m1)
    w1 = 1.0 / (1.0 + e)
    w2 = e / (1.0 + e)
    o_ref[...] = (jnp.where(lane == 0, i1.astype(F32), 0.0) + jnp.where(lane == 1, i2.astype(F32), 0.0)
                  + jnp.where(lane == 2, w1, 0.0) + jnp.where(lane == 3, w2, 0.0))


def _router(x2, nw, sc, sh, rw_pad, seq):
    m, d = x2.shape
    tm = ROUTER_TM
    per_b = seq // tm
    return pl.pallas_call(
        _router_kernel,
        grid=(m // tm,),
        in_specs=[pl.BlockSpec((tm, d), lambda i: (i, 0)),
                  pl.BlockSpec((1, d), lambda i: (0, 0)),
                  pl.BlockSpec((1, 1, d), lambda i: (i // per_b, 0, 0)),
                  pl.BlockSpec((1, 1, d), lambda i: (i // per_b, 0, 0)),
                  pl.BlockSpec((d, LANE), lambda i: (0, 0))],
        out_specs=[pl.BlockSpec((tm, LANE), lambda i: (i, 0)), pl.BlockSpec((tm, d // LANE, LANE), lambda i: (i, 0, 0))],
        out_shape=[jax.ShapeDtypeStruct((m, LANE), F32), jax.ShapeDtypeStruct((m, d // LANE, LANE), F32)],
        compiler_params=_cparams(("parallel",)),
        name="router",
    )(x2, nw, sc, sh, rw_pad)


MOE_TM = 448
MOE_NF = 7


def _moe_plan(rt, m):
    tm = MOE_TM
    n_assign = TOP_K * m
    nt = -(-n_assign // tm) + N_EXPERTS
    e_all = rt[:, :TOP_K].astype(jnp.int32).reshape(-1)
    onehot = (e_all[:, None] == jnp.arange(N_EXPERTS, dtype=jnp.int32)[None, :]).astype(jnp.int32)
    csum = jnp.cumsum(onehot, axis=0)
    rank = jnp.take_along_axis(csum, e_all[:, None], axis=1)[:, 0] - 1
    counts = csum[-1]
    padded = ((counts + tm - 1) // tm) * tm
    ends = jnp.cumsum(padded)
    dest = (ends - padded)[e_all] + rank
    src = jnp.full((nt * tm,), -1, jnp.int32).at[dest].set(jnp.arange(n_assign, dtype=jnp.int32))
    tile_expert = jnp.minimum(jnp.searchsorted(ends, jnp.arange(nt, dtype=jnp.int32) * tm, side="right"),
                              N_EXPERTS - 1).astype(jnp.int32)
    src = src.reshape(nt, tm)
    valid = src >= 0
    gather_tok = jnp.where(valid, src // TOP_K, 0)
    dump_row = m + jnp.arange(tm, dtype=jnp.int32)[None, :]
    tile_par = (jnp.arange(nt, dtype=jnp.int32) % 2)[:, None]
    srow = jnp.concatenate([dump_row, jnp.where(valid, src // TOP_K, dump_row)], axis=0)
    scol = jnp.concatenate([jnp.ones((1, tm), jnp.int32), jnp.where(valid, src % TOP_K, tile_par)], axis=0)
    return (gather_tok.reshape(nt, 1, tm), srow.reshape(nt + 1, 1, tm), scol.reshape(nt + 1, 1, tm), tile_expert)


def _moe_kernel(te_ref, srow_ref, scol_ref, cur_ref, nxt_ref, h_hbm, w1_ref, w3_ref, w2_ref, y_hbm,
                gbuf, obuf, hb_ref, acc_ref, gsem, ssem):
    t = pl.program_id(0)
    j = pl.program_id(1)
    nt = pl.num_programs(0) - 1
    tm = MOE_TM
    nc = hb_ref.shape[1] // LANE
    per_step = tm // MOE_NF
    slot = t % 2

    def gather_copy(idx_ref, r, sl):
        return pltpu.make_async_copy(h_hbm.at[idx_ref[0, 0, r]], gbuf.at[sl, r], gsem.at[sl])

    def scatter_copy(r, sl):
        return pltpu.make_async_copy(obuf.at[sl, r], y_hbm.at[srow_ref[0, 0, r], scol_ref[0, 0, r]], ssem.at[sl])

    def wait_gather(sl):
        pltpu.make_async_copy(gbuf.at[sl], gbuf.at[sl], gsem.at[sl]).wait()

    def wait_scatter(sl):
        pltpu.make_async_copy(obuf.at[sl], obuf.at[sl], ssem.at[sl]).wait()

    @pl.when((t == 0) & (j == 0))
    def _():
        def body(r, c):
            gather_copy(cur_ref, r, 0).start()
            return c
        lax.fori_loop(0, tm, body, 0)
        obuf[1] = jnp.zeros(obuf.shape[1:], F32)

    @pl.when(j == 0)
    def _():
        wait_gather(slot)
        g = jnp.swapaxes(gbuf[slot], 0, 1)
        for c in range(nc):
            hb_ref[:, c * LANE:(c + 1) * LANE] = g[c].astype(BF16)
        acc_ref[...] = jnp.zeros_like(acc_ref)

    base = j * per_step
    for i in range(per_step):
        gather_copy(nxt_ref, base + i, 1 - slot).start()
        scatter_copy(base + i, 1 - slot).start()
    h = hb_ref[...]
    a = _dot(h, w1_ref[0])
    b = _dot(h, w3_ref[0])
    acc_ref[...] += _dot((_silu(a) * b).astype(BF16), w2_ref[0])

    @pl.when(j == MOE_NF - 1)
    def _():
        @pl.when(t > 0)
        def _():
            wait_scatter(slot)

        a3 = jnp.stack([acc_ref[:, c * LANE:(c + 1) * LANE] for c in range(nc)], axis=0)
        obuf[slot] = jnp.swapaxes(a3, 0, 1)

        @pl.when(t == nt)
        def _():
            wait_scatter(1 - slot)
            wait_gather(1 - slot)


def _moe(h, gather_tok, srow, scol, tile_expert, w1, w3, w2):
    m, nc, _ = h.shape
    d = nc * LANE
    ne, _, f = w1.shape
    tm, tf = MOE_TM, f // MOE_NF
    nt = gather_tok.shape[0]

    def smem_tile(idx):
        return pl.BlockSpec((1, 1, tm), idx, memory_space=pltpu.SMEM)

    def tile(t):
        return jnp.minimum(t, nt - 1)

    grid_spec = pltpu.PrefetchScalarGridSpec(
        num_scalar_prefetch=1,
        grid=(nt + 1, MOE_NF),
        in_specs=[smem_tile(lambda t, j, te: (t, 0, 0)),
                  smem_tile(lambda t, j, te: (t, 0, 0)),
                  smem_tile(lambda t, j, te: (tile(t), 0, 0)),
                  smem_tile(lambda t, j, te: (tile(t + 1), 0, 0)),
                  pl.BlockSpec(memory_space=pl.ANY),
                  pl.BlockSpec((1, d, tf), lambda t, j, te: (te[tile(t)], 0, j)),
                  pl.BlockSpec((1, d, tf), lambda t, j, te: (te[tile(t)], 0, j)),
                  pl.BlockSpec((1, tf, d), lambda t, j, te: (te[tile(t)], j, 0))],
        out_specs=pl.BlockSpec(memory_space=pl.ANY),
        scratch_shapes=[pltpu.VMEM((2, tm, nc, LANE), F32), pltpu.VMEM((2, tm, nc, LANE), F32),
                        pltpu.VMEM((tm, d), BF16), pltpu.VMEM((tm, d), F32),
                        pltpu.SemaphoreType.DMA((2,)), pltpu.SemaphoreType.DMA((2,))])
    return pl.pallas_call(
        _moe_kernel,
        grid_spec=grid_spec,
        out_shape=jax.ShapeDtypeStruct((m + tm, TOP_K, nc, LANE), F32),
        compiler_params=_cparams(("arbitrary", "arbitrary")),
        name="moe",
    )(tile_expert, srow, scol, gather_tok, gather_tok, h, w1, w3, w2)


COMBINE_TM = 512


def _combine_kernel(y_ref, rt_ref, x_ref, g2_ref, nwo_ref, o_ref):
    nc = y_ref.shape[2]
    rt = rt_ref[...]
    y1 = jnp.swapaxes(y_ref[:, 0], 0, 1)
    y2 = jnp.swapaxes(y_ref[:, 1], 0, 1)
    y = rt[:, 2:3] * jnp.concatenate([y1[c] for c in range(nc)], axis=1) \
        + rt[:, 3:4] * jnp.concatenate([y2[c] for c in range(nc)], axis=1)
    o_ref[...] = x_ref[...] + g2_ref[0] * _rms_rows(y, nwo_ref[...])


def _combine(y2, rt, x2, g2, nwo, seq):
    m, d = x2.shape
    tm = COMBINE_TM
    per_b = seq // tm
    return pl.pallas_call(
        _combine_kernel,
        grid=(m // tm,),
        in_specs=[pl.BlockSpec((tm, TOP_K, d // LANE, LANE), lambda i: (i, 0, 0, 0)),
                  pl.BlockSpec((tm, LANE), lambda i: (i, 0)),
                  pl.BlockSpec((tm, d), lambda i: (i, 0)),
                  pl.BlockSpec((1, 1, d), lambda i: (i // per_b, 0, 0)),
                  pl.BlockSpec((1, d), lambda i: (0, 0))],
        out_specs=pl.BlockSpec((tm, d), lambda i: (i, 0)),
        out_shape=jax.ShapeDtypeStruct((m, d), F32),
        compiler_params=_cparams(("parallel",)),
        name="moe_combine",
    )(y2, rt, x2, g2, nwo)


def _reorder_w_in(w):
    o = _IN_OFF

    def cols(a, b):
        return w[:, o[a]:o[b]]

    small = jnp.concatenate([cols(4, 6), cols(13, 17)], axis=1)
    pad = jnp.zeros((w.shape[0], P_COLS - P_SMALL - small.shape[1]), w.dtype)
    out = jnp.concatenate([cols(6, 9),
                           cols(2, 4),
                           cols(0, 2),
                           cols(9, 13),
                           cols(17, 18),
                           small, pad], axis=1)
    return out.astype(BF16)


def _gla_gate_params(up, bias):
    hi = up.astype(BF16)
    lo = (up - hi.astype(F32)).astype(BF16)
    out = jnp.zeros((2, 2 * LANE, GLA_HEADS * GLA_DK), BF16)
    for dd, off in ((0, S_GWF), (1, S_GWB)):
        out = out.at[dd, off:off + GLA_RANK].set(hi[dd])
        out = out.at[dd, LANE + off:LANE + off + GLA_RANK].set(lo[dd])
    return out, bias.reshape(2, 1, -1).astype(F32)


def _gdn_gate_params(a_log, dt_bias):
    h = GDN_HEADS
    sel = np.zeros((2, LANE, LANE), np.float32)
    for dd, (boff, aoff) in enumerate(((S_NBF, S_NAF), (S_NBB, S_NAB))):
        for i in range(h):
            sel[dd, boff + i, i] = 1.0
            sel[dd, aoff + i, h + i] = 1.0
    nega = jnp.zeros((2, 1, LANE), F32).at[:, 0, h:2 * h].set(-jnp.exp(a_log.astype(F32)))
    dtb = jnp.zeros((2, 1, LANE), F32).at[:, 0, h:2 * h].set(dt_bias.astype(F32))
    return jnp.asarray(sel, BF16), nega, dtb


def _rope_inv_row():
    inv = ROPE_THETA ** (-jnp.arange(0, ROT_DIM, 2, dtype=F32) / ROT_DIM)
    lane = np.arange(LANE) % DIFF_DH
    row = jnp.where(jnp.asarray(lane < ROT_DIM), inv[jnp.asarray(lane % (ROT_DIM // 2))], 0.0)
    return row.reshape(1, LANE).astype(F32)


def kernel(x, c, positions, adaln_w, adaln_b, norm_w, w_in, gla_gate_up, gla_gate_bias, gla_norm, diff_lambda,
           diff_norm, gdn_conv, gdn_A_log, gdn_dt_bias, gdn_norm, w_branch, w_out, ffn_w1, ffn_w3, ffn_w2,
           router_w, moe_w1, moe_w3, moe_w2):
    batch, seq, d = x.shape
    depth = adaln_w.shape[0]
    m = batch * seq

    c_pad = jnp.zeros((8, d), F32).at[:batch].set(c)
    mod = _adaln(c_pad, adaln_w, adaln_b)[:, :batch]
    pos3 = positions.reshape(batch, seq, 1)
    inv_row = _rope_inv_row()

    for layer in range(depth):
        sh1, sc1, g1, sh2, sc2, g2 = [t.reshape(batch, 1, d) for t in jnp.split(mod[layer], 6, axis=-1)]
        nw = norm_w[layer].astype(F32)

        p = _inproj(x.reshape(m, d), nw[0:1], sc1, sh1, _reorder_w_in(w_in[layer]), seq)
        p3 = p.reshape(batch, seq, P_COLS)

        up2, gbias = _gla_gate_params(gla_gate_up[layer], gla_gate_bias[layer])
        gla_o = _gla(p3, up2, gbias, batch, seq)

        qk = _rope(p3, pos3, inv_row, batch, seq)
        lam_init = 0.8 - 0.6 * math.exp(-0.3 * layer)
        att_o = _attn(qk, p3, diff_lambda[layer].astype(F32), lam_init, batch, seq)

        qkv = _gdn_prep(p3, gdn_conv[layer].astype(F32), batch, seq)
        sel_m, nega, dtb = _gdn_gate_params(gdn_A_log[layer], gdn_dt_bias[layer])
        gdn_o = _gdn(qkv, p3, sel_m, nega, dtb, batch, seq)

        nrm3 = jnp.stack([jnp.tile(gla_norm[layer], 4), jnp.tile(diff_norm[layer], 4),
                          jnp.tile(gdn_norm[layer], 4)]).astype(F32)
        x = _merge(gla_o, p3, att_o, gdn_o, x, g1, nrm3, w_branch[layer].astype(BF16),
                   w_out[layer].astype(BF16), nw[1:2], 1.0 - lam_init, batch, seq)

        x2 = x.reshape(m, d)
        if layer % 2 == 0:
            li = layer // 2
            x2 = _ffn(x2, nw[2:3], sc2, sh2, ffn_w1[li].astype(BF16), ffn_w3[li].astype(BF16),
                      ffn_w2[li].astype(BF16), g2, nw[3:4], seq)
        else:
            li = layer // 2
            rw_pad = jnp.zeros((d, LANE), F32).at[:, :N_EXPERTS].set(router_w[li])
            rt, h2 = _router(x2, nw[2:3], sc2, sh2, rw_pad, seq)
            y = _moe(h2, *_moe_plan(rt, m), moe_w1[li].astype(BF16), moe_w3[li].astype(BF16),
                     moe_w2[li].astype(BF16))
            x2 = _combine(y, rt, x2, g2, nw[3:4], seq)
        x = x2.reshape(batch, seq, d)
    return x
```

```python
import functools
import math

import jax
import jax.numpy as jnp
import numpy as np
from jax import lax
from jax.experimental import pallas as pl
from jax.experimental.pallas import tpu as pltpu

F32 = jnp.float32
BF16 = jnp.bfloat16

D_MODEL = 1024
GLA_HEADS, GLA_DK, GLA_DV, GLA_RANK, GLA_TAU = 4, 64, 128, 16, 16.0
DIFF_HEADS, DIFF_DH = 4, 64
ROT_DIM = DIFF_DH // 4
ROPE_THETA = 500000.0
GDN_HEADS, GDN_DK, GDN_DV, CONV_W = 4, 128, 128, 4
CHUNK = 64
N_BRANCH = 3
N_EXPERTS, TOP_K = 8, 2
EPS = 1e-6
BRANCH_W = 512

_IN_SIZES = (256, 256, 512, 512, 16, 16, 512, 512, 512, 512, 512, 512, 512, 4, 4, 4, 4, 3 * D_MODEL)
_IN_OFF = tuple(int(v) for v in np.concatenate([[0], np.cumsum(_IN_SIZES)]))

P_DQ, P_DK, P_DV = 0, 512, 1024
P_GV, P_GR = 1536, 2048
P_GQ, P_GK = 2560, 2816
P_NQ, P_NK, P_NV, P_NZ = 3072, 3584, 4096, 4608
P_MG = 5120
P_SMALL = 8192
P_COLS = 8448
S_GWF, S_GWB, S_NBF, S_NBB, S_NAF, S_NAB = 0, 16, 32, 36, 40, 44

VMEM_LIMIT = 56 * 1024 * 1024
LANE = 128


def _cparams(sem):
    return pltpu.CompilerParams(dimension_semantics=sem, vmem_limit_bytes=VMEM_LIMIT)


def _dot(a, b):
    return jnp.dot(a, b, preferred_element_type=F32)


def _dot_nt(a, b):
    return lax.dot_general(a, b, (((1,), (1,)), ((), ())), preferred_element_type=F32)


def _dot_tn(a, b):
    return lax.dot_general(a, b, (((0,), (0,)), ((), ())), preferred_element_type=F32)


def _split3_rows(x):
    hi = x.astype(BF16)
    r1 = x - hi.astype(F32)
    mid = r1.astype(BF16)
    lo = (r1 - mid.astype(F32)).astype(BF16)
    return jnp.concatenate([hi, mid, lo], axis=0)


def _dot_split(a, b):
    a_hi = a.astype(BF16)
    a_lo = (a - a_hi.astype(F32)).astype(BF16)
    b_hi = b.astype(BF16)
    b_lo = (b - b_hi.astype(F32)).astype(BF16)
    return _dot(jnp.concatenate([a_hi, a_lo, a_hi], axis=1), jnp.concatenate([b_hi, b_hi, b_lo], axis=0))


def _sigmoid(x):
    return 1.0 / (1.0 + jnp.exp(-x))


def _silu(x):
    return x * _sigmoid(x)


def _softplus(x):
    return jnp.maximum(x, 0.0) + jnp.log1p(jnp.exp(-jnp.abs(x)))


def _rms_rows(x, w):
    return x * lax.rsqrt(jnp.mean(x * x, axis=-1, keepdims=True) + EPS) * w


def _tri_masks(d):
    r = lax.broadcasted_iota(jnp.int32, (CHUNK, CHUNK), 0)
    c = lax.broadcasted_iota(jnp.int32, (CHUNK, CHUNK), 1)
    lag = (r - c) * (1 - 2 * d)
    return lag >= 0, lag > 0


def _adaln_kernel(c_ref, w_ref, b_ref, o_ref):
    c = c_ref[...]
    o_ref[0] = jnp.dot(_silu(c), w_ref[0], preferred_element_type=F32,
                       precision=lax.Precision.HIGHEST) + b_ref[0]


def _adaln(c_pad, adaln_w, adaln_b):
    depth, d, n = adaln_w.shape
    tn = 1536
    return pl.pallas_call(
        _adaln_kernel,
        grid=(depth, n // tn),
        in_specs=[pl.BlockSpec((8, d), lambda l, j: (0, 0)),
                  pl.BlockSpec((1, d, tn), lambda l, j: (l, 0, j)),
                  pl.BlockSpec((1, 1, tn), lambda l, j: (l, 0, j))],
        out_specs=pl.BlockSpec((1, 8, tn), lambda l, j: (l, 0, j)),
        out_shape=jax.ShapeDtypeStruct((depth, 8, n), F32),
        compiler_params=_cparams(("parallel", "parallel")),
        name="adaln",
    )(c_pad, adaln_w, adaln_b.reshape(depth, 1, n))


def _inproj_kernel(x_ref, nw_ref, sc_ref, sh_ref, w_ref, o_ref):
    h = (_rms_rows(x_ref[...], nw_ref[...]) * (1.0 + sc_ref[0]) + sh_ref[0]).astype(BF16)
    o_ref[...] = _dot(h, w_ref[...]).astype(o_ref.dtype)


def _resident(shape):
    return pl.BlockSpec(shape, lambda *_: (0,) * len(shape), pipeline_mode=pl.Buffered(1))


def _inproj(x2, nw, sc, sh, w, seq):
    m, d = x2.shape
    n = w.shape[1]
    tm = 512
    per_b = seq // tm
    return pl.pallas_call(
        _inproj_kernel,
        grid=(m // tm,),
        in_specs=[pl.BlockSpec((tm, d), lambda i: (i, 0)),
                  pl.BlockSpec((1, d), lambda i: (0, 0)),
                  pl.BlockSpec((1, 1, d), lambda i: (i // per_b, 0, 0)),
                  pl.BlockSpec((1, 1, d), lambda i: (i // per_b, 0, 0)),
                  _resident((d, n))],
        out_specs=pl.BlockSpec((tm, n), lambda i: (i, 0)),
        out_shape=jax.ShapeDtypeStruct((m, n), BF16),
        compiler_params=_cparams(("parallel",)),
        name="inproj",
    )(x2, nw, sc, sh, w)


GLA_TL = 1024
GLA_CB = 16


def _gla_kernel(q_ref, k_ref, v_ref, sm_ref, up_ref, bias_ref, o_ref, st_ref):
    d = pl.program_id(0)
    nc = GLA_TL // CHUNK

    @pl.when(pl.program_id(2) == 0)
    def _():
        st_ref[...] = jnp.zeros_like(st_ref)

    incl, _ = _tri_masks(d)
    tri = jnp.where(incl, 1.0, 0.0).astype(BF16)
    tri3 = jnp.concatenate([tri, tri, tri], axis=1)
    fwd = d == 0
    scale = GLA_DK ** -0.5

    H = GLA_HEADS
    ks = [slice(h * GLA_DK, (h + 1) * GLA_DK) for h in range(H)]
    vs = [slice(h * GLA_DV, (h + 1) * GLA_DV) for h in range(H)]

    def group(gi, carry):
        rows = []
        for ci in range(GLA_CB):
            c = gi * GLA_CB + ci
            ce = jnp.where(fwd, c, nc - 1 - c)
            rows.append(pl.ds(pl.multiple_of(ce * CHUNK, CHUNK), CHUNK))
        cr = range(GLA_CB)
        sm = [sm_ref[0, r, :] for r in rows]
        z = [_dot(jnp.concatenate([x, x], axis=1), up_ref[0]) + bias_ref[0] for x in sm]
        g = [(jnp.minimum(x, 0.0) - jnp.log1p(jnp.exp(-jnp.abs(x)))) * (1.0 / GLA_TAU) for x in z]
        b = [_dot(tri3, _split3_rows(x)) for x in g]
        b_last = [jnp.where(fwd, x[CHUNK - 1:CHUNK], x[0:1]) for x in b]
        b_mid = [jnp.where(fwd, x[CHUNK // 2 - 1:CHUNK // 2], x[CHUNK // 2:CHUNK // 2 + 1]) for x in b]
        q = [q_ref[0, r, :].astype(F32) * scale for r in rows]
        k = [k_ref[0, r, :].astype(F32) for r in rows]
        qe = [(q[i] * jnp.exp(b[i] - b_mid[i])).astype(BF16) for i in cr]
        ke = [(k[i] * jnp.exp(b_mid[i] - b[i])).astype(BF16) for i in cr]
        kend = [(k[i] * jnp.exp(b_last[i] - b[i])).astype(BF16) for i in cr]
        qd = [(q[i] * jnp.exp(b[i])).astype(BF16) for i in cr]
        dec = [jnp.exp(x) for x in b_last]
        ch = [(ci, h) for ci in cr for h in range(H)]
        v_a = [v_ref[0, rows[ci], vs[h]] for ci, h in ch]
        a = [jnp.where(incl, _dot_nt(qe[ci][:, ks[h]], ke[ci][:, ks[h]]), 0.0).astype(BF16) for ci, h in ch]
        o_in = [_dot(a[i], v_a[i]) for i in range(len(ch))]
        kv = [_dot_tn(v_a[i], kend[ci][:, ks[h]]) for i, (ci, h) in enumerate(ch)]
        st = st_ref[...]
        for ci in cr:
            st_b = st.astype(BF16)
            for h in range(H):
                o = o_in[ci * H + h] + _dot_nt(qd[ci][:, ks[h]], st_b[:, ks[h]])
                o_ref[0, 0, rows[ci], vs[h]] = o.astype(o_ref.dtype)
            st = st * dec[ci] + jnp.concatenate([kv[ci * H + h] for h in range(H)], axis=1)
        st_ref[...] = st
        return carry

    lax.fori_loop(0, nc // GLA_CB, group, 0)


def _gla(p3, up2, bias, batch, seq):
    nblk = seq // GLA_TL

    def blk(dd, j):
        return j + dd * (nblk - 1 - 2 * j)

    return pl.pallas_call(
        _gla_kernel,
        grid=(2, batch, nblk),
        in_specs=[pl.BlockSpec((1, GLA_TL, 256), lambda dd, b, j: (b, blk(dd, j), P_GQ // 256)),
                  pl.BlockSpec((1, GLA_TL, 256), lambda dd, b, j: (b, blk(dd, j), P_GK // 256)),
                  pl.BlockSpec((1, GLA_TL, 512), lambda dd, b, j: (b, blk(dd, j), P_GV // 512)),
                  pl.BlockSpec((1, GLA_TL, LANE), lambda dd, b, j: (b, blk(dd, j), P_SMALL // LANE)),
                  pl.BlockSpec((1, 2 * LANE, 256), lambda dd, b, j: (dd, 0, 0)),
                  pl.BlockSpec((1, 1, 256), lambda dd, b, j: (dd, 0, 0))],
        out_specs=pl.BlockSpec((1, 1, GLA_TL, 512), lambda dd, b, j: (dd, b, blk(dd, j), 0)),
        out_shape=jax.ShapeDtypeStruct((2, batch, seq, 512), BF16),
        scratch_shapes=[pltpu.VMEM((GLA_DV, GLA_HEADS * GLA_DK), F32)],
        compiler_params=_cparams(("parallel", "parallel", "arbitrary")),
        name="gla",
    )(p3, p3, p3, p3, up2, bias)


ROPE_TL = 1024


def _rope_kernel(q_ref, k_ref, pos_ref, inv_ref, o_ref):
    ang = pos_ref[0].astype(F32) * inv_ref[...]
    cosf = jnp.cos(ang)
    sinf = jnp.sin(ang)
    lane = lax.broadcasted_iota(jnp.int32, (1, LANE), 1) % DIFF_DH
    half = ROT_DIM // 2
    cmul = jnp.where(lane < ROT_DIM, cosf, 1.0)
    s_up = jnp.where(lane < half, -sinf, 0.0)
    s_dn = jnp.where((lane >= half) & (lane < ROT_DIM), sinf, 0.0)
    for s, (x_ref, scale) in enumerate(((q_ref, DIFF_DH ** -0.5 * math.log2(math.e)), (k_ref, 1.0))):
        for gidx in range(4):
            cs = slice(gidx * LANE, (gidx + 1) * LANE)
            x = x_ref[0, :, cs].astype(F32)
            y = x * cmul + pltpu.roll(x, LANE - half, 1) * s_up + pltpu.roll(x, half, 1) * s_dn
            o_ref[s, 0, :, cs] = (y * scale).astype(o_ref.dtype)


def _rope(p3, pos3, inv_row, batch, seq):
    nblk = seq // ROPE_TL
    return pl.pallas_call(
        _rope_kernel,
        grid=(batch, nblk),
        in_specs=[pl.BlockSpec((1, ROPE_TL, 512), lambda b, j: (b, j, P_DQ // 512)),
                  pl.BlockSpec((1, ROPE_TL, 512), lambda b, j: (b, j, P_DK // 512)),
                  pl.BlockSpec((1, ROPE_TL, 1), lambda b, j: (b, j, 0)),
                  pl.BlockSpec((1, LANE), lambda b, j: (0, 0))],
        out_specs=pl.BlockSpec((2, 1, ROPE_TL, 512), lambda b, j: (0, b, j, 0)),
        out_shape=jax.ShapeDtypeStruct((2, batch, seq, 512), BF16),
        compiler_params=_cparams(("parallel", "parallel")),
        name="rope",
    )(p3, p3, pos3, inv_row)


ATT_TQ = 512
ATT_TK = 2048


def _attn_kernel(q_ref, k_ref, v_ref, lam_ref, o_ref, sa_ref, sb_ref, *, lam_init):
    q = q_ref[0, 0]
    lane = lax.broadcasted_iota(jnp.int32, q.shape, 1)
    zero = jnp.zeros_like(q)
    q1 = jnp.where(lane < DIFF_DH, q, zero)
    q2 = jnp.where(lane >= DIFF_DH, q, zero)
    tq = q.shape[0]
    tk = sa_ref.shape[2]
    nk = k_ref.shape[2] // tk
    ones_col = jnp.where(lax.broadcasted_iota(jnp.int32, (tk, LANE), 1) == 0, 1.0, 0.0).astype(BF16)

    def soft(s, v1, m, l, acc):
        m_new = jnp.maximum(m, jnp.max(s, axis=-1, keepdims=True))
        p = jnp.exp2((s - m_new).astype(BF16))
        alpha = jnp.exp2(m - m_new)
        pv = _dot(p, v1)
        return m_new, alpha * l + pv[:, LANE:LANE + 1], alpha * acc + pv[:, :LANE]

    def k_tile(t):
        return k_ref[0, 0, pl.ds(t * tk, tk), :]

    def v_tile(t):
        return jnp.concatenate([v_ref[0, pl.ds(t * tk, tk), :], ones_col], axis=1)

    def scores(t, dst):
        k = k_tile(t)
        dst[0] = _dot_nt(q1, k)
        dst[1] = _dot_nt(q2, k)

    def consume(t, src, carry):
        m1, l1, a1, m2, l2, a2 = carry
        v1 = v_tile(t)
        m1, l1, a1 = soft(src[0], v1, m1, l1, a1)
        m2, l2, a2 = soft(src[1], v1, m2, l2, a2)
        return m1, l1, a1, m2, l2, a2

    neg = jnp.full((tq, 1), -jnp.inf, F32)
    zl = jnp.zeros((tq, 1), F32)
    za = jnp.zeros((tq, LANE), F32)
    carry = (neg, zl, za, neg, zl, za)
    bufs = (sa_ref, sb_ref)
    scores(0, bufs[0])
    for t in range(nk):
        if t + 1 < nk:
            scores(t + 1, bufs[(t + 1) % 2])
        carry = consume(t, bufs[t % 2], carry)
    m1, l1, a1, m2, l2, a2 = carry
    lp = lam_ref[...]
    lam = (jnp.exp(jnp.sum(lp[0:1] * lp[1:2], axis=-1, keepdims=True))
           - jnp.exp(jnp.sum(lp[2:3] * lp[3:4], axis=-1, keepdims=True)) + lam_init)
    o_ref[0] = (a1 / l1 - lam * (a2 / l2)).astype(o_ref.dtype)


def _attn(qk, p3, lam_p, lam_init, batch, seq):
    nq = seq // ATT_TQ
    tk = min(ATT_TK, seq // 2)
    return pl.pallas_call(
        functools.partial(_attn_kernel, lam_init=lam_init),
        grid=(batch, DIFF_HEADS, nq),
        in_specs=[pl.BlockSpec((1, 1, ATT_TQ, LANE), lambda b, h, i: (0, b, i, h)),
                  pl.BlockSpec((1, 1, seq, LANE), lambda b, h, i: (1, b, 0, h)),
                  pl.BlockSpec((1, seq, LANE), lambda b, h, i: (b, 0, P_DV // LANE + h)),
                  pl.BlockSpec((4, DIFF_DH), lambda b, h, i: (0, 0))],
        out_specs=pl.BlockSpec((1, ATT_TQ, LANE), lambda b, h, i: (b, i, h)),
        out_shape=jax.ShapeDtypeStruct((batch, seq, 512), BF16),
        scratch_shapes=[pltpu.VMEM((2, ATT_TQ, tk), F32), pltpu.VMEM((2, ATT_TQ, tk), F32)],
        compiler_params=_cparams(("parallel", "parallel", "arbitrary")),
        name="diffattn",
    )(qk, qk, p3, lam_p)


GPREP_TL = 1024
HALO = 16


def _gdn_prep_kernel(x_ref, prev_ref, next_ref, w_ref, o_ref, xe_ref):
    j = pl.program_id(1)
    s = pl.program_id(2)
    nblk = pl.num_programs(1)
    tl = GPREP_TL
    prev = prev_ref[0].astype(F32)
    nxt = next_ref[0].astype(F32)
    xe_ref[0:HALO, :] = jnp.where(j == 0, 0.0, prev)
    xe_ref[HALO:HALO + tl, :] = x_ref[0].astype(F32)
    xe_ref[HALO + tl:2 * HALO + tl, :] = jnp.where(j == nblk - 1, 0.0, nxt)
    w = w_ref[...]
    y = (xe_ref[pl.ds(HALO - 1, tl), :] * w[0:1] + xe_ref[pl.ds(HALO, tl), :] * w[1:2]
         + xe_ref[pl.ds(HALO + 1, tl), :] * w[2:3] + xe_ref[pl.ds(HALO + 2, tl), :] * w[3:4])
    y = _silu(y)
    qscale = jnp.where(s == 0, GDN_DK ** -0.5, 1.0)
    for h in range(GDN_HEADS):
        cs = slice(h * GDN_DK, (h + 1) * GDN_DK)
        yh = y[:, cs]
        nrm = lax.rsqrt(jnp.sum(yh * yh, axis=-1, keepdims=True) + EPS) * qscale
        fac = jnp.where(s == 2, 1.0, nrm)
        o_ref[0, 0, :, cs] = (yh * fac).astype(o_ref.dtype)


def _gdn_prep(p3, conv_w, batch, seq):
    nblk = seq // GPREP_TL
    hb = GPREP_TL // HALO
    last = seq // HALO - 1
    base = P_NQ // 512
    return pl.pallas_call(
        _gdn_prep_kernel,
        grid=(batch, nblk, 3),
        in_specs=[pl.BlockSpec((1, GPREP_TL, 512), lambda b, j, s: (b, j, base + s)),
                  pl.BlockSpec((1, HALO, 512), lambda b, j, s: (b, jnp.maximum(j * hb - 1, 0), base + s)),
                  pl.BlockSpec((1, HALO, 512), lambda b, j, s: (b, jnp.minimum((j + 1) * hb, last), base + s)),
                  pl.BlockSpec((CONV_W, 512), lambda b, j, s: (0, s))],
        out_specs=pl.BlockSpec((1, 1, GPREP_TL, 512), lambda b, j, s: (s, b, j, 0)),
        out_shape=jax.ShapeDtypeStruct((3, batch, seq, 512), BF16),
        scratch_shapes=[pltpu.VMEM((GPREP_TL + 2 * HALO, 512), F32)],
        compiler_params=_cparams(("parallel", "parallel", "arbitrary")),
        name="gdn_prep",
    )(p3, p3, p3, conv_w)


GDN_TL = 512
GDN_CB = 8


def _gdn_kernel(q_ref, k_ref, v_ref, sm_ref, sel_ref, nega_ref, dtb_ref, o_ref, s_ref):
    d = pl.program_id(0)
    nc = GDN_TL // CHUNK

    @pl.when(pl.program_id(2) == 0)
    def _():
        s_ref[...] = jnp.zeros_like(s_ref)

    incl, strict = _tri_masks(d)
    tri = jnp.where(incl, 1.0, 0.0).astype(BF16)
    tri3 = jnp.concatenate([tri, tri, tri], axis=1)
    upper = jnp.where(strict, 1.0, 0.0)
    r_i = lax.broadcasted_iota(jnp.int32, (CHUNK, CHUNK), 0)
    c_i = lax.broadcasted_iota(jnp.int32, (CHUNK, CHUNK), 1)
    eye = jnp.where(r_i == c_i, 1.0, 0.0)
    fwd = d == 0
    H = GDN_HEADS

    def group(gi, carry):
        rows = []
        for ci in range(GDN_CB):
            c = gi * GDN_CB + ci
            ce = jnp.where(fwd, c, nc - 1 - c)
            rows.append(pl.ds(pl.multiple_of(ce * CHUNK, CHUNK), CHUNK))
        sel = [_dot(sm_ref[0, r, :], sel_ref[0]) for r in rows]
        beta = [_sigmoid(x) for x in sel]
        g = [nega_ref[0] * _softplus(x + dtb_ref[0]) for x in sel]
        g2 = [jnp.concatenate([x[:, H + h:H + h + 1] * upper for h in range(H)] + [x], axis=1) for x in g]
        dd = [_dot(tri3, _split3_rows(x)) for x in g2]
        ch = [(ci, h) for ci in range(GDN_CB) for h in range(H)]
        hs = [slice(h * GDN_DK, (h + 1) * GDN_DK) for h in range(H)]
        k_a = [k_ref[0, 0, rows[ci], hs[h]] for ci, h in ch]
        q_a = [q_ref[0, 0, rows[ci], hs[h]] for ci, h in ch]
        v_a = [v_ref[0, 0, rows[ci], hs[h]] for ci, h in ch]
        n_ch = range(len(ch))
        kf = [k_a[i].astype(F32) for i in n_ch]
        gam = [jnp.where(incl, jnp.exp(dd[ci][:, h * CHUNK:(h + 1) * CHUNK]), 0.0) for ci, h in ch]
        bcol = [dd[ci][:, H * CHUNK + H + h:H * CHUNK + H + h + 1] for ci, h in ch]
        blast = [jnp.where(fwd, b[CHUNK - 1:CHUNK], b[0:1]) for b in bcol]
        eb = [jnp.exp(b) for b in bcol]
        bet = [beta[ci][:, h:h + 1] for ci, h in ch]
        kk = [_dot_nt(k_a[i], k_a[i]) for i in n_ch]
        qk = [_dot_nt(q_a[i], k_a[i]) for i in n_ch]
        pw = [jnp.where(strict, -(bet[i] * kk[i] * gam[i]), 0.0) for i in n_ch]
        t = [eye + pw[i] for i in n_ch]
        for _ in range(5):
            pw = [_dot_split(pw[i], pw[i]) for i in n_ch]
            t = [t[i] + _dot_split(t[i], pw[i]) for i in n_ch]
        rhs = [jnp.concatenate([v_a[i].astype(F32) * bet[i], kf[i] * (bet[i] * eb[i])], axis=1).astype(BF16)
               for i in n_ch]
        sol = [_dot(t[i].astype(BF16), rhs[i]) for i in n_ch]
        u = [sol[i][:, :GDN_DV] for i in n_ch]
        w = [sol[i][:, GDN_DV:].astype(BF16) for i in n_ch]
        attn = [(qk[i] * gam[i]).astype(BF16) for i in n_ch]
        qdec = [(q_a[i].astype(F32) * eb[i]).astype(BF16) for i in n_ch]
        kend = [(kf[i] * jnp.exp(blast[i] - bcol[i])).astype(BF16) for i in n_ch]
        dec = [jnp.exp(b) for b in blast]
        st = [s_ref[h] for h in range(H)]
        for ci in range(GDN_CB):
            ix = [ci * H + h for h in range(H)]
            st_b = [st[h].astype(BF16) for h in range(H)]
            vnb = [(u[i] - _dot(w[i], st_b[h])).astype(BF16) for h, i in enumerate(ix)]
            o_new = [(_dot(qdec[i], st_b[h]) + _dot(attn[i], vnb[h])).astype(o_ref.dtype) for h, i in enumerate(ix)]
            st = [dec[i] * st[h] + _dot_tn(kend[i], vnb[h]) for h, i in enumerate(ix)]
            for h in range(H):
                o_ref[0, 0, rows[ci], hs[h]] = o_new[h]
        for h in range(H):
            s_ref[h] = st[h]
        return carry

    lax.fori_loop(0, nc // GDN_CB, group, 0)


def _gdn(qkv, p3, sel_m, nega, dtb, batch, seq):
    nblk = seq // GDN_TL

    def blk(dd, j):
        return j + dd * (nblk - 1 - 2 * j)

    def qspec(idx):
        return pl.BlockSpec((1, 1, GDN_TL, 512), lambda dd, b, j: (idx, b, blk(dd, j), 0))

    return pl.pallas_call(
        _gdn_kernel,
        grid=(2, batch, nblk),
        in_specs=[qspec(0), qspec(1), qspec(2),
                  pl.BlockSpec((1, GDN_TL, LANE), lambda dd, b, j: (b, blk(dd, j), P_SMALL // LANE)),
                  pl.BlockSpec((1, LANE, LANE), lambda dd, b, j: (dd, 0, 0)),
                  pl.BlockSpec((1, 1, LANE), lambda dd, b, j: (dd, 0, 0)),
                  pl.BlockSpec((1, 1, LANE), lambda dd, b, j: (dd, 0, 0))],
        out_specs=pl.BlockSpec((1, 1, GDN_TL, 512), lambda dd, b, j: (dd, b, blk(dd, j), 0)),
        out_shape=jax.ShapeDtypeStruct((2, batch, seq, 512), BF16),
        scratch_shapes=[pltpu.VMEM((GDN_HEADS, GDN_DK, GDN_DV), F32)],
        compiler_params=_cparams(("parallel", "parallel", "arbitrary")),
        name="gdn",
    )(qkv, qkv, qkv, p3, sel_m, nega, dtb)


MERGE_TM = 512


def _head_rms(o, w):
    parts = []
    for h in range(4):
        oh = o[:, h * LANE:(h + 1) * LANE]
        parts.append(oh * lax.rsqrt(jnp.mean(oh * oh, axis=-1, keepdims=True) + EPS))
    return jnp.concatenate(parts, axis=1) * w


def _merge_kernel(gf_ref, gb_ref, gr_ref, od_ref, nf_ref, nb_ref, nz_ref, mg0_ref, mg1_ref, mg2_ref,
                  x_ref, g1_ref, nrm_ref, wb_ref, wo_ref, nw_ref, o_ref, *, diff_scale):
    nrm = nrm_ref[...]
    y_gla = _head_rms(gf_ref[0, 0].astype(F32) + gb_ref[0, 0].astype(F32), nrm[0:1]) * _silu(gr_ref[0].astype(F32))
    y_diff = _head_rms(od_ref[0].astype(F32), nrm[1:2]) * diff_scale
    y_gdn = _head_rms(nf_ref[0, 0].astype(F32) + nb_ref[0, 0].astype(F32), nrm[2:3]) * _silu(nz_ref[0].astype(F32))
    merged = (_sigmoid(mg0_ref[0].astype(F32)) * _dot(y_gla.astype(BF16), wb_ref[0])
              + _sigmoid(mg1_ref[0].astype(F32)) * _dot(y_diff.astype(BF16), wb_ref[1])
              + _sigmoid(mg2_ref[0].astype(F32)) * _dot(y_gdn.astype(BF16), wb_ref[2]))
    y = _dot(merged.astype(BF16), wo_ref[...])
    o_ref[0] = x_ref[0] + g1_ref[0] * _rms_rows(y, nw_ref[...])


def _merge(gla_o, p3, att_o, gdn_o, x3, g1, nrm3, wb, wo, nw, diff_scale, batch, seq):
    tm = MERGE_TM
    nblk = seq // tm
    d = D_MODEL

    def dirspec(idx):
        return pl.BlockSpec((1, 1, tm, 512), lambda b, j: (idx, b, j, 0))

    def pspec(width, col):
        return pl.BlockSpec((1, tm, width), lambda b, j: (b, j, col // width))

    return pl.pallas_call(
        functools.partial(_merge_kernel, diff_scale=diff_scale),
        grid=(batch, nblk),
        in_specs=[dirspec(0), dirspec(1), pspec(512, P_GR),
                  pl.BlockSpec((1, tm, 512), lambda b, j: (b, j, 0)),
                  dirspec(0), dirspec(1), pspec(512, P_NZ),
                  pspec(d, P_MG), pspec(d, P_MG + d), pspec(d, P_MG + 2 * d),
                  pl.BlockSpec((1, tm, d), lambda b, j: (b, j, 0)),
                  pl.BlockSpec((1, 1, d), lambda b, j: (b, 0, 0)),
                  pl.BlockSpec((3, 512), lambda b, j: (0, 0)),
                  pl.BlockSpec((3, 512, d), lambda b, j: (0, 0, 0)),
                  pl.BlockSpec((d, d), lambda b, j: (0, 0)),
                  pl.BlockSpec((1, d), lambda b, j: (0, 0))],
        out_specs=pl.BlockSpec((1, tm, d), lambda b, j: (b, j, 0)),
        out_shape=jax.ShapeDtypeStruct((batch, seq, d), F32),
        compiler_params=_cparams(("parallel", "parallel")),
        name="merge",
    )(gla_o, gla_o, p3, att_o, gdn_o, gdn_o, p3, p3, p3, p3, x3, g1, nrm3, wb, wo, nw)


FFN_TM = 512


def _ffn_kernel(x_ref, nw_ref, sc_ref, sh_ref, w1_ref, w3_ref, w2_ref, g2_ref, nwo_ref, o_ref):
    x = x_ref[...]
    h = (_rms_rows(x, nw_ref[...]) * (1.0 + sc_ref[0]) + sh_ref[0]).astype(BF16)
    a = _dot(h, w1_ref[...])
    b = _dot(h, w3_ref[...])
    y = _dot((_silu(a) * b).astype(BF16), w2_ref[...])
    o_ref[...] = x + g2_ref[0] * _rms_rows(y, nwo_ref[...])


def _ffn(x2, nw, sc, sh, w1, w3, w2, g2, nwo, seq):
    m, d = x2.shape
    f = w1.shape[1]
    tm = FFN_TM
    per_b = seq // tm
    vec = pl.BlockSpec((1, d), lambda i: (0, 0))
    mod = pl.BlockSpec((1, 1, d), lambda i: (i // per_b, 0, 0))
    return pl.pallas_call(
        _ffn_kernel,
        grid=(m // tm,),
        in_specs=[pl.BlockSpec((tm, d), lambda i: (i, 0)), vec, mod, mod,
                  _resident((d, f)), _resident((d, f)), _resident((f, d)), mod, vec],
        out_specs=pl.BlockSpec((tm, d), lambda i: (i, 0)),
        out_shape=jax.ShapeDtypeStruct((m, d), F32),
        compiler_params=_cparams(("parallel",)),
        name="ffn",
    )(x2, nw, sc, sh, w1, w3, w2, g2, nwo)


ROUTER_TM = 1024


def _router_kernel(x_ref, nw_ref, sc_ref, sh_ref, rw_ref, o_ref, h_ref):
    y = _rms_rows(x_ref[...], nw_ref[...])
    h = y * (1.0 + sc_ref[0]) + sh_ref[0]
    h_ref[...] = jnp.swapaxes(jnp.stack([h[:, c * LANE:(c + 1) * LANE] for c in range(h.shape[1] // LANE)], axis=0), 0, 1)
    logits = jnp.dot(h, rw_ref[...], preferred_element_type=F32, precision=lax.Precision.HIGHEST)
    lane = lax.broadcasted_iota(jnp.int32, logits.shape, 1)
    neg = -jnp.inf
    logits = jnp.where(lane < N_EXPERTS, logits, neg)
    m1 = jnp.max(logits, axis=-1, keepdims=True)
    i1 = jnp.min(jnp.where(logits == m1, lane, LANE), axis=-1, keepdims=True)
    rest = jnp.where(lane == i1, neg, logits)
    m2 = jnp.max(rest, axis=-1, keepdims=True)
    i2 = jnp.min(jnp.where(rest == m2, lane, LANE), axis=-1, keepdims=True)
    e = jnp.exp(m2 - m1)
    w1 = 1.0 / (1.0 + e)
    w2 = e / (1.0 + e)
    o_ref[...] = (jnp.where(lane == 0, i1.astype(F32), 0.0) + jnp.where(lane == 1, i2.astype(F32), 0.0)
                  + jnp.where(lane == 2, w1, 0.0) + jnp.where(lane == 3, w2, 0.0))


def _router(x2, nw, sc, sh, rw_pad, seq):
    m, d = x2.shape
    tm = ROUTER_TM
    per_b = seq // tm
    return pl.pallas_call(
        _router_kernel,
        grid=(m // tm,),
        in_specs=[pl.BlockSpec((tm, d), lambda i: (i, 0)),
                  pl.BlockSpec((1, d), lambda i: (0, 0)),
                  pl.BlockSpec((1, 1, d), lambda i: (i // per_b, 0, 0)),
                  pl.BlockSpec((1, 1, d), lambda i: (i // per_b, 0, 0)),
                  pl.BlockSpec((d, LANE), lambda i: (0, 0))],
        out_specs=[pl.BlockSpec((tm, LANE), lambda i: (i, 0)), pl.BlockSpec((tm, d // LANE, LANE), lambda i: (i, 0, 0))],
        out_shape=[jax.ShapeDtypeStruct((m, LANE), F32), jax.ShapeDtypeStruct((m, d // LANE, LANE), F32)],
        compiler_params=_cparams(("parallel",)),
        name="router",
    )(x2, nw, sc, sh, rw_pad)


MOE_TM = 448
MOE_NF = 2


def _moe_plan(rt, m):
    tm = MOE_TM
    n_assign = TOP_K * m
    nt = -(-n_assign // tm) + N_EXPERTS
    e_all = rt[:, :TOP_K].astype(jnp.int32).reshape(-1)
    onehot = (e_all[:, None] == jnp.arange(N_EXPERTS, dtype=jnp.int32)[None, :]).astype(jnp.int32)
    csum = jnp.cumsum(onehot, axis=0)
    rank = jnp.take_along_axis(csum, e_all[:, None], axis=1)[:, 0] - 1
    counts = csum[-1]
    padded = ((counts + tm - 1) // tm) * tm
    ends = jnp.cumsum(padded)
    dest = (ends - padded)[e_all] + rank
    src = jnp.full((nt * tm,), -1, jnp.int32).at[dest].set(jnp.arange(n_assign, dtype=jnp.int32))
    tile_expert = jnp.minimum(jnp.searchsorted(ends, jnp.arange(nt, dtype=jnp.int32) * tm, side="right"),
                              N_EXPERTS - 1).astype(jnp.int32)
    src = src.reshape(nt, tm)
    valid = src >= 0
    gather_tok = jnp.where(valid, src // TOP_K, 0)
    dump_row = m + jnp.arange(tm, dtype=jnp.int32)[None, :]
    tile_par = (jnp.arange(nt, dtype=jnp.int32) % 2)[:, None]
    srow = jnp.concatenate([dump_row, jnp.where(valid, src // TOP_K, dump_row)], axis=0)
    scol = jnp.concatenate([jnp.ones((1, tm), jnp.int32), jnp.where(valid, src % TOP_K, tile_par)], axis=0)
    return (gather_tok.reshape(nt, 1, tm), srow.reshape(nt + 1, 1, tm), scol.reshape(nt + 1, 1, tm), tile_expert)


def _moe_kernel(te_ref, srow_ref, scol_ref, cur_ref, nxt_ref, h_hbm, w1_ref, w3_ref, w2_ref, y_hbm,
                gbuf, obuf, hb_ref, acc_ref, gsem, ssem):
    t = pl.program_id(0)
    j = pl.program_id(1)
    nt = pl.num_programs(0) - 1
    tm = MOE_TM
    nc = hb_ref.shape[1] // LANE
    per_step = tm // MOE_NF
    slot = t % 2

    def gather_copy(idx_ref, r, sl):
        return pltpu.make_async_copy(h_hbm.at[idx_ref[0, 0, r]], gbuf.at[sl, r], gsem.at[sl])

    def scatter_copy(r, sl):
        return pltpu.make_async_copy(obuf.at[sl, r], y_hbm.at[srow_ref[0, 0, r], scol_ref[0, 0, r]], ssem.at[sl])

    def wait_gather(sl):
        pltpu.make_async_copy(gbuf.at[sl], gbuf.at[sl], gsem.at[sl]).wait()

    def wait_scatter(sl):
        pltpu.make_async_copy(obuf.at[sl], obuf.at[sl], ssem.at[sl]).wait()

    @pl.when((t == 0) & (j == 0))
    def _():
        def body(r, c):
            gather_copy(cur_ref, r, 0).start()
            return c
        lax.fori_loop(0, tm, body, 0)
        obuf[1] = jnp.zeros(obuf.shape[1:], F32)

    @pl.when(j == 0)
    def _():
        wait_gather(slot)
        g = jnp.swapaxes(gbuf[slot], 0, 1)
        for c in range(nc):
            hb_ref[:, c * LANE:(c + 1) * LANE] = g[c].astype(BF16)
        acc_ref[...] = jnp.zeros_like(acc_ref)

    base = j * per_step
    for i in range(per_step):
        gather_copy(nxt_ref, base + i, 1 - slot).start()
        scatter_copy(base + i, 1 - slot).start()
    h = hb_ref[...]
    a = _dot(h, w1_ref[0])
    b = _dot(h, w3_ref[0])
    acc_ref[...] += _dot((_silu(a) * b).astype(BF16), w2_ref[0])

    @pl.when(j == MOE_NF - 1)
    def _():
        @pl.when(t > 0)
        def _():
            wait_scatter(slot)

        a3 = jnp.stack([acc_ref[:, c * LANE:(c + 1) * LANE] for c in range(nc)], axis=0)
        obuf[slot] = jnp.swapaxes(a3, 0, 1)

        @pl.when(t == nt)
        def _():
            wait_scatter(1 - slot)
            wait_gather(1 - slot)


def _moe(h, gather_tok, srow, scol, tile_expert, w1, w3, w2):
    m, nc, _ = h.shape
    d = nc * LANE
    ne, _, f = w1.shape
    tm, tf = MOE_TM, f // MOE_NF
    nt = gather_tok.shape[0]

    def smem_tile(idx):
        return pl.BlockSpec((1, 1, tm), idx, memory_space=pltpu.SMEM)

    def tile(t):
        return jnp.minimum(t, nt - 1)

    grid_spec = pltpu.PrefetchScalarGridSpec(
        num_scalar_prefetch=1,
        grid=(nt + 1, MOE_NF),
        in_specs=[smem_tile(lambda t, j, te: (t, 0, 0)),
                  smem_tile(lambda t, j, te: (t, 0, 0)),
                  smem_tile(lambda t, j, te: (tile(t), 0, 0)),
                  smem_tile(lambda t, j, te: (tile(t + 1), 0, 0)),
                  pl.BlockSpec(memory_space=pl.ANY),
                  pl.BlockSpec((1, d, tf), lambda t, j, te: (te[tile(t)], 0, j)),
                  pl.BlockSpec((1, d, tf), lambda t, j, te: (te[tile(t)], 0, j)),
                  pl.BlockSpec((1, tf, d), lambda t, j, te: (te[tile(t)], j, 0))],
        out_specs=pl.BlockSpec(memory_space=pl.ANY),
        scratch_shapes=[pltpu.VMEM((2, tm, nc, LANE), F32), pltpu.VMEM((2, tm, nc, LANE), F32),
                        pltpu.VMEM((tm, d), BF16), pltpu.VMEM((tm, d), F32),
                        pltpu.SemaphoreType.DMA((2,)), pltpu.SemaphoreType.DMA((2,))])
    return pl.pallas_call(
        _moe_kernel,
        grid_spec=grid_spec,
        out_shape=jax.ShapeDtypeStruct((m + tm, TOP_K, nc, LANE), F32),
        compiler_params=_cparams(("arbitrary", "arbitrary")),
        name="moe",
    )(tile_expert, srow, scol, gather_tok, gather_tok, h, w1, w3, w2)


COMBINE_TM = 512


def _combine_kernel(y_ref, rt_ref, x_ref, g2_ref, nwo_ref, o_ref):
    nc = y_ref.shape[2]
    rt = rt_ref[...]
    y1 = jnp.swapaxes(y_ref[:, 0], 0, 1)
    y2 = jnp.swapaxes(y_ref[:, 1], 0, 1)
    y = rt[:, 2:3] * jnp.concatenate([y1[c] for c in range(nc)], axis=1) \
        + rt[:, 3:4] * jnp.concatenate([y2[c] for c in range(nc)], axis=1)
    o_ref[...] = x_ref[...] + g2_ref[0] * _rms_rows(y, nwo_ref[...])


def _combine(y2, rt, x2, g2, nwo, seq):
    m, d = x2.shape
    tm = COMBINE_TM
    per_b = seq // tm
    return pl.pallas_call(
        _combine_kernel,
        grid=(m // tm,),
        in_specs=[pl.BlockSpec((tm, TOP_K, d // LANE, LANE), lambda i: (i, 0, 0, 0)),
                  pl.BlockSpec((tm, LANE), lambda i: (i, 0)),
                  pl.BlockSpec((tm, d), lambda i: (i, 0)),
                  pl.BlockSpec((1, 1, d), lambda i: (i // per_b, 0, 0)),
                  pl.BlockSpec((1, d), lambda i: (0, 0))],
        out_specs=pl.BlockSpec((tm, d), lambda i: (i, 0)),
        out_shape=jax.ShapeDtypeStruct((m, d), F32),
        compiler_params=_cparams(("parallel",)),
        name="moe_combine",
    )(y2, rt, x2, g2, nwo)


def _reorder_w_in(w):
    o = _IN_OFF

    def cols(a, b):
        return w[:, o[a]:o[b]]

    small = jnp.concatenate([cols(4, 6), cols(13, 17)], axis=1)
    pad = jnp.zeros((w.shape[0], P_COLS - P_SMALL - small.shape[1]), w.dtype)
    out = jnp.concatenate([cols(6, 9),
                           cols(2, 4),
                           cols(0, 2),
                           cols(9, 13),
                           cols(17, 18),
                           small, pad], axis=1)
    return out.astype(BF16)


def _gla_gate_params(up, bias):
    hi = up.astype(BF16)
    lo = (up - hi.astype(F32)).astype(BF16)
    out = jnp.zeros((2, 2 * LANE, GLA_HEADS * GLA_DK), BF16)
    for dd, off in ((0, S_GWF), (1, S_GWB)):
        out = out.at[dd, off:off + GLA_RANK].set(hi[dd])
        out = out.at[dd, LANE + off:LANE + off + GLA_RANK].set(lo[dd])
    return out, bias.reshape(2, 1, -1).astype(F32)


def _gdn_gate_params(a_log, dt_bias):
    h = GDN_HEADS
    sel = np.zeros((2, LANE, LANE), np.float32)
    for dd, (boff, aoff) in enumerate(((S_NBF, S_NAF), (S_NBB, S_NAB))):
        for i in range(h):
            sel[dd, boff + i, i] = 1.0
            sel[dd, aoff + i, h + i] = 1.0
    nega = jnp.zeros((2, 1, LANE), F32).at[:, 0, h:2 * h].set(-jnp.exp(a_log.astype(F32)))
    dtb = jnp.zeros((2, 1, LANE), F32).at[:, 0, h:2 * h].set(dt_bias.astype(F32))
    return jnp.asarray(sel, BF16), nega, dtb


def _rope_inv_row():
    inv = ROPE_THETA ** (-jnp.arange(0, ROT_DIM, 2, dtype=F32) / ROT_DIM)
    lane = np.arange(LANE) % DIFF_DH
    row = jnp.where(jnp.asarray(lane < ROT_DIM), inv[jnp.asarray(lane % (ROT_DIM // 2))], 0.0)
    return row.reshape(1, LANE).astype(F32)


def kernel(x, c, positions, adaln_w, adaln_b, norm_w, w_in, gla_gate_up, gla_gate_bias, gla_norm, diff_lambda,
           diff_norm, gdn_conv, gdn_A_log, gdn_dt_bias, gdn_norm, w_branch, w_out, ffn_w1, ffn_w3, ffn_w2,
           router_w, moe_w1, moe_w3, moe_w2):
    batch, seq, d = x.shape
    depth = adaln_w.shape[0]
    m = batch * seq

    c_pad = jnp.zeros((8, d), F32).at[:batch].set(c)
    mod = _adaln(c_pad, adaln_w, adaln_b)[:, :batch]
    pos3 = positions.reshape(batch, seq, 1)
    inv_row = _rope_inv_row()

    for layer in range(depth):
        sh1, sc1, g1, sh2, sc2, g2 = [t.reshape(batch, 1, d) for t in jnp.split(mod[layer], 6, axis=-1)]
        nw = norm_w[layer].astype(F32)

        p = _inproj(x.reshape(m, d), nw[0:1], sc1, sh1, _reorder_w_in(w_in[layer]), seq)
        p3 = p.reshape(batch, seq, P_COLS)

        up2, gbias = _gla_gate_params(gla_gate_up[layer], gla_gate_bias[layer])
        gla_o = _gla(p3, up2, gbias, batch, seq)

        qk = _rope(p3, pos3, inv_row, batch, seq)
        lam_init = 0.8 - 0.6 * math.exp(-0.3 * layer)
        att_o = _attn(qk, p3, diff_lambda[layer].astype(F32), lam_init, batch, seq)

        qkv = _gdn_prep(p3, gdn_conv[layer].astype(F32), batch, seq)
        sel_m, nega, dtb = _gdn_gate_params(gdn_A_log[layer], gdn_dt_bias[layer])
        gdn_o = _gdn(qkv, p3, sel_m, nega, dtb, batch, seq)

        nrm3 = jnp.stack([jnp.tile(gla_norm[layer], 4), jnp.tile(diff_norm[layer], 4),
                          jnp.tile(gdn_norm[layer], 4)]).astype(F32)
        x = _merge(gla_o, p3, att_o, gdn_o, x, g1, nrm3, w_branch[layer].astype(BF16),
                   w_out[layer].astype(BF16), nw[1:2], 1.0 - lam_init, batch, seq)

        x2 = x.reshape(m, d)
        if layer % 2 == 0:
            li = layer // 2
            x2 = _ffn(x2, nw[2:3], sc2, sh2, ffn_w1[li].astype(BF16), ffn_w3[li].astype(BF16),
                      ffn_w2[li].astype(BF16), g2, nw[3:4], seq)
        else:
            li = layer // 2
            rw_pad = jnp.zeros((d, LANE), F32).at[:, :N_EXPERTS].set(router_w[li])
            rt, h2 = _router(x2, nw[2:3], sc2, sh2, rw_pad, seq)
            y = _moe(h2, *_moe_plan(rt, m), moe_w1[li].astype(BF16), moe_w3[li].astype(BF16),
                     moe_w2[li].astype(BF16))
            x2 = _combine(y, rt, x2, g2, nw[3:4], seq)
        x = x2.reshape(batch, seq, d)
    return x
```
